```python
import math
import jax
import jax.numpy as jnp
from jax import lax
import numpy as np

D_MODEL = 1024
BATCH = 8
SEQ = 4096
DEPTH = 2
DEC_BATCH = 32
DEC_SEQ = 1
PAST_LEN = 16384
PAGE_SIZE = 128

SSD_HEAD_DIM = 64
SSD_INNER = 3 * D_MODEL // 8
SSD_HEADS = SSD_INNER // SSD_HEAD_DIM
SSD_GROUPS = 2
SSD_STATE = 64
SSD_CONV = 4
SSD_XBC = SSD_INNER + 2 * SSD_GROUPS * SSD_STATE
SSD_CHUNK = 128
DA_DQK = 32
DA_DV = 2 * DA_DQK
DA_WIDTH = 3 * D_MODEL // 8
DA_HEADS = DA_WIDTH // DA_DV
DA_QK_WIDTH = DA_HEADS * 2 * DA_DQK
Q_BLOCK = 128
S5_WIDTH = D_MODEL - SSD_INNER - DA_WIDTH
S5_GROUP_CH = 16
S5_GROUPS = S5_WIDTH // S5_GROUP_CH
S5_STATE = 64
OFF_XBC = SSD_INNER
OFF_DT = OFF_XBC + SSD_XBC
OFF_U = OFF_DT + SSD_HEADS
OFF_Q = OFF_U + S5_WIDTH
OFF_K = OFF_Q + DA_QK_WIDTH
OFF_V = OFF_K + DA_QK_WIDTH
N_IN = OFF_V + DA_WIDTH
D_MIX = SSD_INNER + S5_WIDTH + DA_WIDTH
MOE_GROUPS = 4
EXPERTS_PER_GROUP = 4
N_EXPERTS = MOE_GROUPS * EXPERTS_PER_GROUP
MOE_TOP_K = 2
EXPERT_HIDDEN = D_MODEL // 2
RMS_EPS = 1e-6

kernel_name = 'hymba_ssd_s5_diffattn_hmoe_step'


def rmsnorm(x, w):
    xf = x.astype(jnp.float32)
    y = xf * lax.rsqrt(jnp.mean(xf * xf, axis=-1, keepdims=True) + RMS_EPS)
    return (y * w.astype(jnp.float32)).astype(x.dtype)


def alibi_slopes():
    return jnp.exp2(-8.0 * jnp.arange(1, DA_HEADS + 1, dtype=jnp.float32) / DA_HEADS)


def causal_conv(xbc, buf, w, b):
    t_len = xbc.shape[1]
    full = jnp.concatenate([buf.astype(xbc.dtype), xbc], axis=1)
    acc = b
    for i in range(SSD_CONV):
        acc = acc + full[:, i:i + t_len, :] * w[i]
    return jax.nn.silu(acc), full[:, t_len:, :]


def ssd_chunked(x, dt, a, bm, cm, h0):
    f32 = jnp.float32
    bsz, t_len, n_h, n_p = x.shape
    blk = SSD_CHUNK if t_len % SSD_CHUNK == 0 else t_len
    nc = t_len // blk
    rep = n_h // bm.shape[2]
    xc = x.astype(f32).reshape(bsz, nc, blk, n_h, n_p)
    bc = jnp.repeat(bm.astype(f32), rep, axis=2).reshape(bsz, nc, blk, n_h, -1)
    cc = jnp.repeat(cm.astype(f32), rep, axis=2).reshape(bsz, nc, blk, n_h, -1)
    dtc = dt.reshape(bsz, nc, blk, n_h)
    acum = jnp.cumsum(dtc * a, axis=2)
    seg = acum[:, :, :, None, :] - acum[:, :, None, :, :]
    causal = jnp.tril(jnp.ones((blk, blk), dtype=bool))[None, None, :, :, None]
    decay = jnp.exp(jnp.where(causal, seg, -jnp.inf))
    scores = jnp.einsum('bclhn,bcshn->bclsh', cc, bc) * decay
    y_diag = jnp.einsum('bclsh,bcsh,bcshp->bclhp', scores, dtc, xc)
    to_end = jnp.exp(acum[:, :, -1:, :] - acum)
    chunk_states = jnp.einsum('bclhn,bclh,bclhp->bchpn', bc, to_end * dtc, xc)
    chunk_decay = jnp.exp(acum[:, :, -1, :])

    def step(h, inp):
        s_c, d_c = inp
        return d_c[:, :, None, None] * h + s_c, h

    h_final, h_in = lax.scan(step, h0, (jnp.moveaxis(chunk_states, 1, 0), jnp.moveaxis(chunk_decay, 1, 0)))
    h_in = jnp.moveaxis(h_in, 0, 1)
    y_off = jnp.einsum('bclhn,bchpn->bclhp', cc, h_in) * jnp.exp(acum)[..., None]
    return (y_diag + y_off).reshape(bsz, t_len, n_h, n_p), h_final


def ssd_mixer(z, xbc, dt_raw, conv_buf, h0, lp):
    f32 = jnp.float32
    bsz, t_len, _ = xbc.shape
    gn = SSD_GROUPS * SSD_STATE
    act, conv_new = causal_conv(xbc, conv_buf, lp['conv_w'], lp['conv_b'])
    xs = act[..., :SSD_INNER].reshape(bsz, t_len, SSD_HEADS, SSD_HEAD_DIM)
    bm = act[..., SSD_INNER:SSD_INNER + gn].reshape(bsz, t_len, SSD_GROUPS, SSD_STATE)
    cm = act[..., SSD_INNER + gn:].reshape(bsz, t_len, SSD_GROUPS, SSD_STATE)
    dt = jax.nn.softplus(dt_raw.astype(f32) + lp['dt_bias'].astype(f32))
    a = -jnp.exp(lp['a_log'].astype(f32))
    y, h_new = ssd_chunked(xs, dt, a, bm, cm, h0.astype(f32))
    y = y + lp['ssd_d'].astype(f32)[:, None] * xs.astype(f32)
    y = y.reshape(bsz, t_len, SSD_INNER) * jax.nn.silu(z.astype(f32))
    return rmsnorm(y, lp['ssd_norm_w']).astype(z.dtype), conv_new, h_new


def _linear_combine(left, right):
    a_l, b_l = left
    a_r, b_r = right
    return a_r * a_l, a_r * b_l + b_r


def s5_mixer(u, s0, lp):
    f32 = jnp.float32
    bsz, t_len, _ = u.shape
    lam = lax.complex(lp['s5_lambda_re'].astype(f32), lp['s5_lambda_im'].astype(f32))
    step = jnp.exp(lp['s5_log_step'].astype(f32))[:, None]
    lam_bar = jnp.exp(lam * step)
    b = lax.complex(lp['s5_b_re'].astype(f32), lp['s5_b_im'].astype(f32))
    b_bar = ((lam_bar - 1.0) / lam)[..., None] * b
    c = lax.complex(lp['s5_c_re'].astype(f32), lp['s5_c_im'].astype(f32))
    ug = u.astype(f32).reshape(bsz, t_len, S5_GROUPS, S5_GROUP_CH)
    bu = jnp.einsum('gnc,btgc->btgn', b_bar, ug.astype(jnp.complex64))
    s0c = lax.complex(s0[..., 0].astype(f32), s0[..., 1].astype(f32))
    bu = bu.at[:, 0].add(lam_bar * s0c)
    decay = jnp.broadcast_to(lam_bar, bu.shape)
    _, states = lax.associative_scan(_linear_combine, (decay, bu), axis=1)
    y = jnp.einsum('gcn,btgn->btgc', c, states).real
    y = y + lp['s5_d'].astype(f32).reshape(S5_GROUPS, S5_GROUP_CH) * ug
    y = jax.nn.gelu(y.reshape(bsz, t_len, S5_WIDTH))
    g = y @ lp['s5_w_glu'].astype(f32) + lp['s5_b_glu'].astype(f32)
    out = g[..., :S5_WIDTH] * jax.nn.sigmoid(g[..., S5_WIDTH:])
    last = states[:, -1]
    return out.astype(u.dtype), jnp.stack([last.real, last.imag], axis=-1)


def diff_attn_prompt(q, k, v, lam, slopes):
    f32 = jnp.float32
    bsz, t_len = q.shape[0], q.shape[1]
    nb = t_len // Q_BLOCK
    scale = DA_DQK ** -0.5
    qb = jnp.swapaxes(q.reshape(bsz, nb, Q_BLOCK, DA_HEADS, 2, DA_DQK), 0, 1)
    kpos = jnp.arange(t_len)
    vf = v.astype(f32)

    def block(args):
        qi, bi = args
        qpos = bi * Q_BLOCK + jnp.arange(Q_BLOCK)
        dist = (qpos[:, None] - kpos[None, :]).astype(f32)
        s = jnp.einsum('bqhme,bkhme->bmhqk', qi, k).astype(f32) * scale - slopes[:, None, None] * dist
        s = jnp.where(dist >= 0, s, -jnp.inf)
        p = jax.nn.softmax(s, axis=-1)
        w = p[:, 0] - lam * p[:, 1]
        return jnp.einsum('bhqk,bkhv->bqhv', w, vf)

    out = lax.map(block, (qb, jnp.arange(nb)))
    return jnp.swapaxes(out, 0, 1).reshape(bsz, t_len, DA_HEADS, DA_DV)


def diff_attn_sample(q, k, v, k_past, v_past, lam, slopes):
    f32 = jnp.float32
    s_len = q.shape[1]
    p_len = k_past.shape[1]
    scale = DA_DQK ** -0.5
    qpos = p_len + jnp.arange(s_len)
    d_past = (qpos[:, None] - jnp.arange(p_len)[None, :]).astype(f32)
    d_new = (jnp.arange(s_len)[:, None] - jnp.arange(s_len)[None, :]).astype(f32)
    s_past = jnp.einsum('bqhme,bkhme->bmhqk', q, k_past).astype(f32) * scale - slopes[:, None, None] * d_past
    s_new = jnp.einsum('bqhme,bkhme->bmhqk', q, k).astype(f32) * scale - slopes[:, None, None] * d_new
    s_new = jnp.where(d_new >= 0, s_new, -jnp.inf)
    p = jax.nn.softmax(jnp.concatenate([s_past, s_new], axis=-1), axis=-1)
    w = p[:, 0] - lam * p[:, 1]
    return (jnp.einsum('bhqk,bkhv->bqhv', w[..., :p_len], v_past.astype(f32))
            + jnp.einsum('bhqk,bkhv->bqhv', w[..., p_len:], v.astype(f32)))


def hier_moe(h, lp):
    f32 = jnp.float32
    bsz, t_len, d = h.shape
    n_tok = bsz * t_len
    t = h.reshape(n_tok, d)
    rows = jnp.arange(n_tok)
    g_logits = (t @ lp['w_router_group']).astype(f32) + lp['b_router_group'].astype(f32)
    g_idx = jnp.argmax(g_logits, axis=-1)
    g_prob = jax.nn.softmax(g_logits, axis=-1)[rows, g_idx][:, None]
    e_logits = ((t @ lp['w_router_expert']).astype(f32) + lp['b_router_expert'].astype(f32)).reshape(n_tok, MOE_GROUPS, EXPERTS_PER_GROUP)
    e_in_group = e_logits[rows, g_idx]
    top_w, top_i = lax.top_k(jax.nn.softmax(e_in_group, axis=-1), MOE_TOP_K)
    top_w = top_w / jnp.sum(top_w, axis=-1, keepdims=True) * g_prob
    expert_id = g_idx[:, None] * EXPERTS_PER_GROUP + top_i
    combine = jnp.sum(jax.nn.one_hot(expert_id, N_EXPERTS, dtype=f32) * top_w[..., None], axis=1)
    out = jnp.zeros((n_tok, d), f32)
    for e in range(N_EXPERTS):
        hid = jax.nn.silu(t @ lp['w_gate'][e]) * (t @ lp['w_up'][e])
        out = out + combine[:, e:e + 1] * (hid @ lp['w_down'][e]).astype(f32)
    return out.reshape(bsz, t_len, d).astype(h.dtype)


def trunk_layer(x, lp, lam_init, conv_buf, ssd_h0, s5_s0, attend):
    f32 = jnp.float32
    bsz, t_len, _ = x.shape
    h = rmsnorm(x, lp['norm1_w'])
    proj = h @ lp['w_in']
    z = proj[..., :OFF_XBC]
    xbc = proj[..., OFF_XBC:OFF_DT]
    dt_raw = proj[..., OFF_DT:OFF_U]
    u = proj[..., OFF_U:OFF_Q]
    q = proj[..., OFF_Q:OFF_K].reshape(bsz, t_len, DA_HEADS, 2, DA_DQK)
    k = proj[..., OFF_K:OFF_V].reshape(bsz, t_len, DA_HEADS, 2, DA_DQK)
    v = proj[..., OFF_V:].reshape(bsz, t_len, DA_HEADS, DA_DV)
    y_ssd, conv_new, ssd_new = ssd_mixer(z, xbc, dt_raw, conv_buf, ssd_h0, lp)
    y_s5, s5_new = s5_mixer(u, s5_s0, lp)
    lam = (jnp.exp(jnp.sum(lp['lam_q1'].astype(f32) * lp['lam_k1'].astype(f32)))
           - jnp.exp(jnp.sum(lp['lam_q2'].astype(f32) * lp['lam_k2'].astype(f32))) + lam_init)
    o = attend(q, k, v, lam)
    y_da = (rmsnorm(o, lp['da_norm_w']) * (1.0 - lam_init)).reshape(bsz, t_len, DA_WIDTH)
    mixed = jnp.concatenate([y_ssd, y_s5, y_da.astype(x.dtype)], axis=-1) @ lp['w_out']
    x = x + mixed
    x = x + hier_moe(rmsnorm(x, lp['norm2_w']), lp)
    k_rows = k.reshape(bsz, t_len, DA_HEADS, 2 * DA_DQK)
    return x, (k_rows, v, conv_new, ssd_new, s5_new)


def setup_inputs(seed: int = 0) -> dict:
    key = jax.random.key(seed)
    ks = iter(jax.random.split(key, 48))
    f32 = jnp.float32

    def nrm(shape, scale):
        return jax.random.normal(next(ks), shape, f32) * scale

    def unif(shape, lo, hi):
        return jax.random.uniform(next(ks), shape, f32, lo, hi)

    n_pages = PAST_LEN // PAGE_SIZE
    n_used = DEC_BATCH * n_pages
    n_pool = n_used + n_used // 4
    perm = jax.random.permutation(next(ks), n_pool)
    page_table = perm[:n_used].reshape(DEC_BATCH, n_pages).astype(jnp.int32)
    dt0 = jnp.exp(unif((DEPTH, SSD_HEADS), math.log(1e-3), math.log(1e-1)))
    n_idx = jnp.arange(S5_STATE, dtype=f32)
    return {
        'x_prompt': nrm((BATCH, SEQ, D_MODEL), 1.0),
        'x_sample': nrm((DEC_BATCH, DEC_SEQ, D_MODEL), 1.0),
        'cache_k': nrm((DEPTH, n_pool, PAGE_SIZE, DA_HEADS, 2 * DA_DQK), 1.0),
        'cache_v': nrm((DEPTH, n_pool, PAGE_SIZE, DA_HEADS, DA_DV), 1.0),
        'page_table': page_table,
        'state_conv': nrm((DEPTH, DEC_BATCH, SSD_CONV - 1, SSD_XBC), 1.0),
        'state_ssd': nrm((DEPTH, DEC_BATCH, SSD_HEADS, SSD_HEAD_DIM, SSD_STATE), 0.1),
        'state_s5': nrm((DEPTH, DEC_BATCH, S5_GROUPS, S5_STATE, 2), 0.5),
        'norm1_w': 1.0 + nrm((DEPTH, D_MODEL), 0.02),
        'w_in': nrm((DEPTH, D_MODEL, N_IN), D_MODEL ** -0.5),
        'conv_w': nrm((DEPTH, SSD_CONV, SSD_XBC), SSD_CONV ** -0.5),
        'conv_b': nrm((DEPTH, SSD_XBC), 0.01),
        'dt_bias': dt0 + jnp.log(-jnp.expm1(-dt0)),
        'a_log': jnp.log(unif((DEPTH, SSD_HEADS), 1.0, 16.0)),
        'ssd_d': 1.0 + nrm((DEPTH, SSD_HEADS), 0.1),
        'ssd_norm_w': 1.0 + nrm((DEPTH, SSD_INNER), 0.02),
        's5_lambda_re': -0.5 + nrm((DEPTH, S5_GROUPS, S5_STATE), 0.01),
        's5_lambda_im': math.pi * n_idx + nrm((DEPTH, S5_GROUPS, S5_STATE), 0.01),
        's5_log_step': unif((DEPTH, S5_GROUPS), math.log(1e-3), math.log(1e-1)),
        's5_b_re': nrm((DEPTH, S5_GROUPS, S5_STATE, S5_GROUP_CH), (2 * S5_GROUP_CH) ** -0.5),
        's5_b_im': nrm((DEPTH, S5_GROUPS, S5_STATE, S5_GROUP_CH), (2 * S5_GROUP_CH) ** -0.5),
        's5_c_re': nrm((DEPTH, S5_GROUPS, S5_GROUP_CH, S5_STATE), (2 * S5_STATE) ** -0.5),
        's5_c_im': nrm((DEPTH, S5_GROUPS, S5_GROUP_CH, S5_STATE), (2 * S5_STATE) ** -0.5),
        's5_d': nrm((DEPTH, S5_WIDTH), 1.0),
        's5_w_glu': nrm((DEPTH, S5_WIDTH, 2 * S5_WIDTH), S5_WIDTH ** -0.5),
        's5_b_glu': nrm((DEPTH, 2 * S5_WIDTH), 0.01),
        'lam_q1': nrm((DEPTH, DA_DQK), 0.1),
        'lam_k1': nrm((DEPTH, DA_DQK), 0.1),
        'lam_q2': nrm((DEPTH, DA_DQK), 0.1),
        'lam_k2': nrm((DEPTH, DA_DQK), 0.1),
        'da_norm_w': 1.0 + nrm((DEPTH, DA_DV), 0.02),
        'w_out': nrm((DEPTH, D_MIX, D_MODEL), D_MIX ** -0.5),
        'norm2_w': 1.0 + nrm((DEPTH, D_MODEL), 0.02),
        'w_router_group': nrm((DEPTH, D_MODEL, MOE_GROUPS), D_MODEL ** -0.5),
        'b_router_group': nrm((DEPTH, MOE_GROUPS), 0.01),
        'w_router_expert': nrm((DEPTH, D_MODEL, N_EXPERTS), D_MODEL ** -0.5),
        'b_router_expert': nrm((DEPTH, N_EXPERTS), 0.01),
        'w_gate': nrm((DEPTH, N_EXPERTS, D_MODEL, EXPERT_HIDDEN), D_MODEL ** -0.5),
        'w_up': nrm((DEPTH, N_EXPERTS, D_MODEL, EXPERT_HIDDEN), D_MODEL ** -0.5),
        'w_down': nrm((DEPTH, N_EXPERTS, EXPERT_HIDDEN, D_MODEL), EXPERT_HIDDEN ** -0.5),
        'final_norm_w': 1.0 + nrm((D_MODEL,), 0.02),
    }


def reference(x_prompt, x_sample, cache_k, cache_v, page_table, state_conv, state_ssd, state_s5,
              norm1_w, w_in, conv_w, conv_b, dt_bias, a_log, ssd_d, ssd_norm_w,
              s5_lambda_re, s5_lambda_im, s5_log_step, s5_b_re, s5_b_im, s5_c_re, s5_c_im,
              s5_d, s5_w_glu, s5_b_glu, lam_q1, lam_k1, lam_q2, lam_k2, da_norm_w, w_out,
              norm2_w, w_router_group, b_router_group, w_router_expert, b_router_expert,
              w_gate, w_up, w_down, final_norm_w):
    f32 = jnp.float32
    slopes = alibi_slopes()
    dec_b, n_pages = page_table.shape
    past_len = n_pages * PAGE_SIZE
    xp = x_prompt
    xs = x_sample
    p_states = []
    s_states = []
    for l in range(DEPTH):
        lp = {
            'norm1_w': norm1_w[l], 'w_in': w_in[l], 'conv_w': conv_w[l], 'conv_b': conv_b[l],
            'dt_bias': dt_bias[l], 'a_log': a_log[l], 'ssd_d': ssd_d[l], 'ssd_norm_w': ssd_norm_w[l],
            's5_lambda_re': s5_lambda_re[l], 's5_lambda_im': s5_lambda_im[l], 's5_log_step': s5_log_step[l],
            's5_b_re': s5_b_re[l], 's5_b_im': s5_b_im[l], 's5_c_re': s5_c_re[l], 's5_c_im': s5_c_im[l],
            's5_d': s5_d[l], 's5_w_glu': s5_w_glu[l], 's5_b_glu': s5_b_glu[l],
            'lam_q1': lam_q1[l], 'lam_k1': lam_k1[l], 'lam_q2': lam_q2[l], 'lam_k2': lam_k2[l],
            'da_norm_w': da_norm_w[l], 'w_out': w_out[l], 'norm2_w': norm2_w[l],
            'w_router_group': w_router_group[l], 'b_router_group': b_router_group[l],
            'w_router_expert': w_router_expert[l], 'b_router_expert': b_router_expert[l],
            'w_gate': w_gate[l], 'w_up': w_up[l], 'w_down': w_down[l],
        }
        lam_init = 0.8 - 0.6 * math.exp(-0.3 * l)
        bp = xp.shape[0]
        xp, st_p = trunk_layer(
            xp, lp, lam_init,
            jnp.zeros((bp, SSD_CONV - 1, SSD_XBC), xp.dtype),
            jnp.zeros((bp, SSD_HEADS, SSD_HEAD_DIM, SSD_STATE), f32),
            jnp.zeros((bp, S5_GROUPS, S5_STATE, 2), f32),
            lambda q, k, v, lam: diff_attn_prompt(q, k, v, lam, slopes))
        p_states.append(st_p)
        k_past = cache_k[l][page_table].reshape(dec_b, past_len, DA_HEADS, 2, DA_DQK)
        v_past = cache_v[l][page_table].reshape(dec_b, past_len, DA_HEADS, DA_DV)
        xs, st_s = trunk_layer(
            xs, lp, lam_init, state_conv[l], state_ssd[l], state_s5[l],
            lambda q, k, v, lam: diff_attn_sample(q, k, v, k_past, v_past, lam, slopes))
        s_states.append(st_s)
    k_p, v_p, conv_p, ssd_p, s5_p = [jnp.stack([st[i] for st in p_states]) for i in range(5)]
    k_s, v_s, conv_s, ssd_s, s5_s = [jnp.stack([st[i] for st in s_states]) for i in range(5)]
    y_prompt = rmsnorm(xp, final_norm_w)
    y_sample = rmsnorm(xs, final_norm_w)
    return (y_prompt, y_sample, k_p, v_p, k_s, v_s, conv_p, conv_s, ssd_p, ssd_s, s5_p, s5_s)
```

```python
import functools
import math

import jax
import jax.numpy as jnp
import numpy as np
from jax import lax
from jax.experimental import pallas as pl
from jax.experimental.pallas import tpu as pltpu

F32 = jnp.float32
BF16 = jnp.bfloat16

D_MODEL = 1024
DEPTH = 2
PAGE_SIZE = 128
SSD_HEAD_DIM = 64
SSD_INNER = 384
SSD_HEADS = 6
SSD_GROUPS = 2
SSD_STATE = 64
SSD_CONV = 4
SSD_XBC = 640
SSD_CHUNK = 128
DA_DQK = 32
DA_DV = 64
DA_WIDTH = 384
DA_HEADS = 6
S5_WIDTH = 256
S5_GROUP_CH = 16
S5_GROUPS = 16
S5_STATE = 64
S5_N = S5_GROUPS * S5_STATE
OFF_XBC = 384
OFF_DT = 1024
OFF_U = 1030
OFF_Q = 1286
OFF_K = 1670
OFF_V = 2054
N_IN = 2438
N_IN_PAD = 2560
MOE_GROUPS = 4
EXPERTS_PER_GROUP = 4
N_EXPERTS = 16
N_CLASSES = 24
EXPERT_HIDDEN = 512
RMS_EPS = 1e-6
LANES = 128
META_W = 128
ROW_W = D_MODEL + META_W
VMEM_LIMIT = 56 * 1024 * 1024
NEG_BIG = -1e30
LOG2E = 1.4426950408889634

_PAIR_LO = (0, 0, 0, 1, 1, 2)
_PAIR_HI = (1, 2, 3, 2, 3, 3)


def _cparams(sem):
    return pltpu.CompilerParams(dimension_semantics=sem, vmem_limit_bytes=VMEM_LIMIT)


def _rms(x, w):
    ms = jnp.mean(x * x, axis=-1, keepdims=True)
    return (x * lax.rsqrt(ms + RMS_EPS)) * w


def _softplus(x):
    return jnp.maximum(x, 0.0) + jnp.log(1.0 + jnp.exp(-jnp.abs(x)))


def _split2(a):
    hi = a.astype(BF16)
    lo = (a - hi.astype(F32)).astype(BF16)
    return hi, lo


def _split2_param(a):
    hi = lax.reduce_precision(a, exponent_bits=8, mantissa_bits=7)
    return hi.astype(BF16), (a - hi).astype(BF16)


def _dot(a, b):
    return jnp.dot(a, b, preferred_element_type=F32)


def _dot_nt(a, b):
    return lax.dot_general(a, b, (((1,), (1,)), ((), ())), preferred_element_type=F32)


def _dot_tn(a, b):
    return lax.dot_general(a, b, (((0,), (0,)), ((), ())), preferred_element_type=F32)


def _mm(a, b, hp, nt=False):
    d = _dot_nt if nt else _dot
    if not hp:
        return d(a, b)
    a_hi, a_lo = _split2(a)
    b_hi, b_lo = _split2(b)
    return d(a_hi, b_hi) + (d(a_lo, b_hi) + d(a_hi, b_lo))


def _dot3(a, b_hi, b_lo):
    a_hi, a_lo = _split2(a)
    return _dot(a_hi, b_hi) + (_dot(a_lo, b_hi) + _dot(a_hi, b_lo))


def _inproj_body(x_ref, nw_ref, *refs, hp):
    h = _rms(x_ref[...], nw_ref[...])
    if hp:
        whi_ref, wlo_ref = refs[:2]
        outs = refs[2:]
        p = _dot3(h, whi_ref[...], wlo_ref[...])
    else:
        outs = refs[1:]
        p = _dot(h.astype(BF16), refs[0][...])
    z_ref, xbc_ref, dt_ref, u_ref, k_ref, v_ref = outs[:6]
    z_ref[...] = p[:, 0:384]
    xbc_ref[...] = p[:, 384:1024]
    dt_ref[...] = p[:, 1024:1152]
    u_ref[...] = p[:, 1152:1408]
    q = p[:, 1408:1792]
    k = p[:, 1792:2176]
    v = p[:, 2176:2560]
    k_ref[...] = k
    v_ref[...] = v
    if hp:
        outs[6][...] = q
    else:
        outs[6][...] = q.astype(BF16)
        outs[7][...] = k.astype(BF16)
        outs[8][...] = v.astype(BF16)


def _inproj(x2d, nw, w_hi, w_lo, tm, hp):
    m = x2d.shape[0]
    widths = (384, 640, 128, 256, 384, 384) + ((384,) if hp else (384, 384, 384))
    dtypes = (F32,) * 6 + ((F32,) if hp else (BF16, BF16, BF16))
    w_spec = pl.BlockSpec((D_MODEL, N_IN_PAD), lambda i: (0, 0))
    weights = (w_hi, w_lo) if hp else (w_hi,)
    return pl.pallas_call(
        functools.partial(_inproj_body, hp=hp),
        grid=(m // tm,),
        in_specs=[
            pl.BlockSpec((tm, D_MODEL), lambda i: (i, 0)),
            pl.BlockSpec((1, D_MODEL), lambda i: (0, 0)),
        ] + [w_spec] * len(weights),
        out_specs=[pl.BlockSpec((tm, w), lambda i: (i, 0)) for w in widths],
        out_shape=[jax.ShapeDtypeStruct((m, w), dt) for w, dt in zip(widths, dtypes)],
        compiler_params=_cparams(("parallel",)),
        name="inproj",
    )(x2d, nw, *weights)


def _ssd_body(xbc_ref, z_ref, dt_ref, cbuf0_ref, h0_ref, cw_ref, cb_ref, dtb_ref, alog_ref, dvec_ref, nw_ref, tri_ref,
              y_ref, cnew_ref, hfin_ref, cbuf, hs, *, t_valid, hp):
    L = SSD_CHUNK
    c = pl.program_id(1)
    nc = pl.num_programs(1)

    @pl.when(c == 0)
    def _():
        cbuf[0:8, :] = cbuf0_ref[0]
        hs[...] = jnp.zeros(hs.shape, F32)
        for h in range(SSD_HEADS):
            g = h // 3
            hs[h * 64:(h + 1) * 64, g * 64:(g + 1) * 64] = h0_ref[0, h]
        cnew_ref[0] = jnp.zeros((8, SSD_XBC), F32)

    cbuf[8:8 + L, :] = xbc_ref[0]
    acc = cb_ref[...] + cbuf[5:5 + L, :] * cw_ref[0:1, :]
    acc = acc + cbuf[6:6 + L, :] * cw_ref[1:2, :]
    acc = acc + cbuf[7:7 + L, :] * cw_ref[2:3, :]
    acc = acc + cbuf[8:8 + L, :] * cw_ref[3:4, :]
    act = jax.nn.silu(acc)

    c_last = (t_valid - 1) // L
    r_last = (t_valid - 1) % L

    @pl.when(c == c_last)
    def _():
        cnew_ref[0, 0:3, :] = cbuf[6 + r_last:9 + r_last, :]

    tail = cbuf[L:L + 8, :]
    cbuf[0:8, :] = tail

    xs = act[:, 0:384]
    bm = act[:, 384:512]
    cm = act[:, 512:640]

    rows = lax.broadcasted_iota(jnp.int32, (L, LANES), 0) + c * L
    dtv = _softplus(dt_ref[0] + dtb_ref[...])
    dtv = jnp.where(rows < t_valid, dtv, 0.0)
    a = -jnp.exp(alog_ref[...])
    dta = dtv * a
    tri = tri_ref[...]
    d_hi = dta.astype(BF16)
    r1 = dta - d_hi.astype(F32)
    d_mid = r1.astype(BF16)
    d_lo = (r1 - d_mid.astype(F32)).astype(BF16)
    acum = _dot(tri, d_hi) + (_dot(tri, d_mid) + _dot(tri, d_lo))
    acum_t = acum.T
    dt_t = dtv.T
    xs_t = xs.T

    lane = lax.broadcasted_iota(jnp.int32, (L, LANES), 1)
    low = lane < 64
    li = lax.broadcasted_iota(jnp.int32, (L, L), 0)
    si = lax.broadcasted_iota(jnp.int32, (L, L), 1)
    causal = li >= si
    pre = (lambda a: a) if hp else (lambda a: a.astype(BF16))
    bm_b = pre(bm)
    cm_b = pre(cm)
    xs_b = pre(xs)
    xs_tb = pre(xs_t)
    cb_g = [
        _mm(pre(jnp.where(low, cm, 0.0)), bm_b, hp, nt=True),
        _mm(pre(jnp.where(low, 0.0, cm)), bm_b, hp, nt=True),
    ]
    lane_h = lax.broadcasted_iota(jnp.int32, (64, LANES), 1)

    y_pairs = []
    for p in range(3):
        halves = []
        ecols = []
        for half in range(2):
            h = 2 * p + half
            g = h // 3
            col = acum[:, h:h + 1]
            row = acum_t[h:h + 1, :]
            decay = jnp.where(causal, jnp.exp(col - row), 0.0)
            mh = (cb_g[g] * decay) * dt_t[h:h + 1, :]
            halves.append(_mm(pre(mh), xs_b[:, p * 128:(p + 1) * 128], hp))
            ecols.append(jnp.exp(col))
        y_diag = jnp.where(low, halves[0], halves[1])
        h_in = hs[p * 128:(p + 1) * 128, :]
        y_off = _mm(cm_b, pre(h_in), hp, nt=True) * jnp.where(low, ecols[0], ecols[1])
        y_pairs.append(y_diag + y_off)
        for half in range(2):
            h = 2 * p + half
            g = h // 3
            col = acum[:, h:h + 1]
            last = acum[L - 1:L, h:h + 1]
            w = jnp.exp(last - col) * dtv[:, h:h + 1]
            st = _mm(xs_tb[h * 64:(h + 1) * 64, :], pre(bm * w), hp)
            gmask = (lane_h < 64) if g == 0 else (lane_h >= 64)
            st = jnp.where(gmask, st, 0.0)
            hs[h * 64:(h + 1) * 64, :] = jnp.exp(last) * hs[h * 64:(h + 1) * 64, :] + st

    y = jnp.concatenate(y_pairs, axis=1) + dvec_ref[...] * xs
    y = y * jax.nn.silu(z_ref[0])
    y_ref[0] = _rms(y, nw_ref[...])

    @pl.when(c == nc - 1)
    def _():
        for h in range(SSD_HEADS):
            g = h // 3
            hfin_ref[0, h] = hs[h * 64:(h + 1) * 64, g * 64:(g + 1) * 64]


def _ssd(xbc, z, dt, cbuf0, h0, cw8, cb, dtb, alog, dvec, nw, tri, t_valid, hp):
    b, tp, _ = xbc.shape
    L = SSD_CHUNK
    const2 = lambda i, j: (0, 0)
    return pl.pallas_call(
        functools.partial(_ssd_body, t_valid=t_valid, hp=hp),
        grid=(b, tp // L),
        in_specs=[
            pl.BlockSpec((1, L, SSD_XBC), lambda i, j: (i, j, 0)),
            pl.BlockSpec((1, L, SSD_INNER), lambda i, j: (i, j, 0)),
            pl.BlockSpec((1, L, LANES), lambda i, j: (i, j, 0)),
            pl.BlockSpec((1, 8, SSD_XBC), lambda i, j: (i, 0, 0)),
            pl.BlockSpec((1, SSD_HEADS, SSD_HEAD_DIM, SSD_STATE), lambda i, j: (i, 0, 0, 0)),
            pl.BlockSpec((8, SSD_XBC), const2),
            pl.BlockSpec((1, SSD_XBC), const2),
            pl.BlockSpec((1, LANES), const2),
            pl.BlockSpec((1, LANES), const2),
            pl.BlockSpec((1, SSD_INNER), const2),
            pl.BlockSpec((1, SSD_INNER), const2),
            pl.BlockSpec((L, L), const2),
        ],
        out_specs=[
            pl.BlockSpec((1, L, SSD_INNER), lambda i, j: (i, j, 0)),
            pl.BlockSpec((1, 8, SSD_XBC), lambda i, j: (i, 0, 0)),
            pl.BlockSpec((1, SSD_HEADS, SSD_HEAD_DIM, SSD_STATE), lambda i, j: (i, 0, 0, 0)),
        ],
        out_shape=[
            jax.ShapeDtypeStruct((b, tp, SSD_INNER), F32),
            jax.ShapeDtypeStruct((b, 8, SSD_XBC), F32),
            jax.ShapeDtypeStruct((b, SSD_HEADS, SSD_HEAD_DIM, SSD_STATE), F32),
        ],
        scratch_shapes=[pltpu.VMEM((L + 8, SSD_XBC), F32), pltpu.VMEM((SSD_INNER, LANES), F32)],
        compiler_params=_cparams(("parallel", "arbitrary")),
        name="ssd",
    )(xbc, z, dt, cbuf0, h0, cw8, cb, dtb, alog, dvec, nw, tri)


def _s5_body(u_ref, s0_ref, are_ref, aim_ref, bhi_ref, blo_ref, chi_ref, clo_ref, dvec_ref, wglu_ref, wglu_lo_ref, bglu_ref,
             y_ref, sfin_ref, st, carry, *, nb, tc, hp):
    i = pl.program_id(0)

    @pl.when(i == 0)
    def _():
        carry[...] = s0_ref[...]

    u = u_ref[...]
    if hp:
        st[...] = _dot3(u, bhi_ref[...], blo_ref[...])
    else:
        st[...] = _dot(u.astype(BF16), bhi_ref[...])
    are = jnp.broadcast_to(are_ref[...], (nb, 2 * S5_N))
    aim = jnp.broadcast_to(aim_ref[...], (nb, 2 * S5_N))

    def step(t, s):
        r = pl.multiple_of(t * nb, nb)
        sw = jnp.concatenate([s[:, S5_N:], s[:, :S5_N]], axis=1)
        s = are * s + aim * sw + st[pl.ds(r, nb), :]
        st[pl.ds(r, nb), :] = s
        return s

    s_last = lax.fori_loop(0, tc, step, carry[...])
    carry[...] = s_last
    sfin_ref[...] = s_last

    if hp:
        y = _dot3(st[...], chi_ref[...], clo_ref[...]) + dvec_ref[...] * u
    else:
        y = _dot(st[...].astype(BF16), chi_ref[...]) + dvec_ref[...] * u
    y = jax.nn.gelu(y)
    if hp:
        g = _dot3(y, wglu_ref[...], wglu_lo_ref[...]) + bglu_ref[...]
    else:
        g = _dot(y.astype(BF16), wglu_ref[...]) + bglu_ref[...]
    y_ref[...] = g[:, :S5_WIDTH] * jax.nn.sigmoid(g[:, S5_WIDTH:])


def _s5(u_tb, s0, are, aim, bhi, blo, chi, clo, dvec, wglu, wglu_lo, bglu, nb, tc, hp):
    rows = u_tb.shape[0]
    blk = tc * nb
    const2 = lambda i: (0, 0)
    return pl.pallas_call(
        functools.partial(_s5_body, nb=nb, tc=tc, hp=hp),
        grid=(rows // blk,),
        in_specs=[
            pl.BlockSpec((blk, S5_WIDTH), lambda i: (i, 0)),
            pl.BlockSpec((nb, 2 * S5_N), const2),
            pl.BlockSpec((1, 2 * S5_N), const2),
            pl.BlockSpec((1, 2 * S5_N), const2),
            pl.BlockSpec((S5_WIDTH, 2 * S5_N), const2),
            pl.BlockSpec((S5_WIDTH, 2 * S5_N), const2),
            pl.BlockSpec((2 * S5_N, S5_WIDTH), const2),
            pl.BlockSpec((2 * S5_N, S5_WIDTH), const2),
            pl.BlockSpec((1, S5_WIDTH), const2),
            pl.BlockSpec((S5_WIDTH, 2 * S5_WIDTH), const2),
            pl.BlockSpec((S5_WIDTH, 2 * S5_WIDTH), const2),
            pl.BlockSpec((1, 2 * S5_WIDTH), const2),
        ],
        out_specs=[
            pl.BlockSpec((blk, S5_WIDTH), lambda i: (i, 0)),
            pl.BlockSpec((nb, 2 * S5_N), const2),
        ],
        out_shape=[
            jax.ShapeDtypeStruct((rows, S5_WIDTH), F32),
            jax.ShapeDtypeStruct((nb, 2 * S5_N), F32),
        ],
        scratch_shapes=[pltpu.VMEM((blk, 2 * S5_N), F32), pltpu.VMEM((nb, 2 * S5_N), F32)],
        compiler_params=_cparams(("arbitrary",)),
        name="s5",
    )(u_tb, s0, are, aim, bhi, blo, chi, clo, dvec, wglu, wglu_lo, bglu)


def _attn_body(qi_ref, kj_ref, q_ref, k_ref, v_ref, slope_ref, lam_ref, dnw_ref, o_ref, m_sc, l_sc, acc_sc, bias_sc,
               *, tq, lam_init, hp):
    p = pl.program_id(1)
    s = pl.program_id(2)
    qi = qi_ref[s]
    kj = kj_ref[s]
    c_qk = (DA_DQK ** -0.5) * LOG2E

    @pl.when(s == 0)
    def _():
        ki = lax.broadcasted_iota(jnp.int32, (tq, tq), 0)
        ql = lax.broadcasted_iota(jnp.int32, (tq, tq), 1)
        rel = (ql - ki).astype(F32)
        for half in range(2):
            bias_sc[half] = (slope_ref[pl.ds(2 * p + half, 1), 0:1] * (-LOG2E)) * rel

    @pl.when(kj == 0)
    def _():
        m_sc[...] = jnp.full(m_sc.shape, NEG_BIG, F32)
        l_sc[...] = jnp.zeros(l_sc.shape, F32)
        acc_sc[...] = jnp.zeros(acc_sc.shape, F32)

    def block(diag):
        q = q_ref[0]
        k = k_ref[0]
        v = v_ref[0]
        qlane = lax.broadcasted_iota(jnp.int32, (tq, LANES), 1)
        zero_q = jnp.zeros_like(q)
        if hp:
            k_hi, k_lo = _split2(k)
            v_hi, v_lo = _split2(v)
            k_cat = jnp.concatenate([k_hi, k_lo, k_hi], axis=1)
        if diag:
            causal = lax.broadcasted_iota(jnp.int32, (tq, tq), 0) <= lax.broadcasted_iota(jnp.int32, (tq, tq), 1)
        off = ((qi - kj) * tq).astype(F32)
        for half in range(2):
            cblk = (slope_ref[pl.ds(2 * p + half, 1), 0:1] * (-LOG2E)) * off
            for mp in range(2):
                cidx = half * 2 + mp
                lo = half * 64 + mp * 32
                qm = jnp.where((qlane >= lo) & (qlane < lo + 32), q, zero_q)
                if hp:
                    qm_hi, qm_lo = _split2(qm)
                    raw = _dot_nt(k_cat, jnp.concatenate([qm_hi, qm_hi, qm_lo], axis=1))
                else:
                    raw = _dot_nt(k, qm)
                sc = raw * c_qk + bias_sc[half]
                if diag:
                    sc = jnp.where(causal, sc, -jnp.inf)
                m_prev = m_sc[cidx:cidx + 1, :]
                m_new = jnp.maximum(m_prev, jnp.max(sc, axis=0, keepdims=True) + cblk)
                alpha = jnp.exp2(m_prev - m_new)
                pr = jnp.exp2(sc - (m_new - cblk))
                l_sc[cidx:cidx + 1, :] = alpha * l_sc[cidx:cidx + 1, :] + jnp.sum(pr, axis=0, keepdims=True)
                m_sc[cidx:cidx + 1, :] = m_new
                if hp:
                    p_hi, p_lo = _split2(pr)
                    pv = _dot_tn(v_hi, p_hi) + (_dot_tn(v_lo, p_hi) + _dot_tn(v_hi, p_lo))
                else:
                    pv = _dot_tn(v, pr.astype(BF16))
                r0 = half * 64
                acc_sc[mp, r0:r0 + 64, :] = alpha * acc_sc[mp, r0:r0 + 64, :] + pv[r0:r0 + 64, :]

    @pl.when(kj != qi)
    def _():
        block(False)

    @pl.when(kj == qi)
    def _():
        block(True)
        lam = lam_ref[0:1, 0:1]
        outs = []
        for half in range(2):
            r0 = half * 64
            o0 = acc_sc[0, r0:r0 + 64, :] / l_sc[half * 2:half * 2 + 1, :]
            o1 = acc_sc[1, r0:r0 + 64, :] / l_sc[half * 2 + 1:half * 2 + 2, :]
            o = o0 - lam * o1
            ms = jnp.mean(o * o, axis=0, keepdims=True)
            outs.append((o * lax.rsqrt(ms + RMS_EPS)) * dnw_ref[...] * (1.0 - lam_init))
        o_ref[0] = jnp.concatenate(outs, axis=0).T


def _attn_prompt(q, k, v, slopes, lam, dnw, lam_init, tq, hp):
    b, t, _ = q.shape
    nq = t // tq
    qi_tab = np.array([i for i in range(nq) for _ in range(i + 1)], np.int32)
    kj_tab = np.array([j for i in range(nq) for j in range(i + 1)], np.int32)
    grid_spec = pltpu.PrefetchScalarGridSpec(
        num_scalar_prefetch=2,
        grid=(b, 3, len(qi_tab)),
        in_specs=[
            pl.BlockSpec((1, tq, LANES), lambda i, p, s, qt, kt: (i, qt[s], p)),
            pl.BlockSpec((1, tq, LANES), lambda i, p, s, qt, kt: (i, kt[s], p)),
            pl.BlockSpec((1, tq, LANES), lambda i, p, s, qt, kt: (i, kt[s], p)),
            pl.BlockSpec((8, LANES), lambda i, p, s, qt, kt: (0, 0)),
            pl.BlockSpec((1, LANES), lambda i, p, s, qt, kt: (0, 0)),
            pl.BlockSpec((64, 1), lambda i, p, s, qt, kt: (0, 0)),
        ],
        out_specs=pl.BlockSpec((1, tq, LANES), lambda i, p, s, qt, kt: (i, qt[s], p)),
        scratch_shapes=[pltpu.VMEM((4, tq), F32), pltpu.VMEM((4, tq), F32), pltpu.VMEM((2, LANES, tq), F32),
                        pltpu.VMEM((2, tq, tq), F32)],
    )
    return pl.pallas_call(
        functools.partial(_attn_body, tq=tq, lam_init=lam_init, hp=hp),
        grid_spec=grid_spec,
        out_shape=jax.ShapeDtypeStruct((b, t, DA_WIDTH), F32),
        compiler_params=_cparams(("parallel", "parallel", "arbitrary")),
        name="attn_prompt",
    )(jnp.asarray(qi_tab), jnp.asarray(kj_tab), q, k, v, slopes, lam, dnw)


_PAGES_PER_STEP = 8


def _attn_dec_body(pt_ref, q_ref, kn_ref, vn_ref, slope_ref, lam_ref, dnw_ref, *refs, past_len, lam_init):
    del pt_ref
    npg = _PAGES_PER_STEP
    k_refs = refs[:npg]
    v_refs = refs[npg:2 * npg]
    o_ref, s_sc, m_sc, l_sc, acc_sc = refs[2 * npg:]
    j = pl.program_id(1)
    nj = pl.num_programs(1)
    nrow = 2 * DA_HEADS

    @pl.when(j == 0)
    def _():
        m_sc[...] = jnp.full(m_sc.shape, NEG_BIG, F32)
        l_sc[...] = jnp.zeros(l_sc.shape, F32)
        acc_sc[...] = jnp.zeros(acc_sc.shape, F32)

    scale = DA_DQK ** -0.5
    lane_f = lax.broadcasted_iota(jnp.int32, (1, PAGE_SIZE), 1).astype(F32)
    mx = [None] * nrow
    for i in range(npg):
        dist = (past_len - (j * npg + i) * PAGE_SIZE).astype(F32) - lane_f
        for h in range(DA_HEADS):
            prod = k_refs[i][h] * q_ref[0, h]
            bias = slope_ref[h:h + 1, :] * dist
            for mp in range(2):
                r = 2 * h + mp
                sc = jnp.sum(prod[32 * mp:32 * mp + 32], axis=0, keepdims=True) * scale - bias
                s_sc[i, r:r + 1, :] = sc
                mx[r] = sc if mx[r] is None else jnp.maximum(mx[r], sc)
    for r in range(nrow):
        h = r // 2
        m_prev = m_sc[r:r + 1, :]
        m_new = jnp.maximum(m_prev, jnp.max(mx[r], axis=1, keepdims=True))
        alpha = jnp.exp(m_prev - m_new)
        m_sc[r:r + 1, :] = m_new
        lsum = alpha * l_sc[r:r + 1, :]
        acc = alpha * acc_sc[r]
        for i in range(npg):
            pr = jnp.exp(s_sc[i, r:r + 1, :] - m_new)
            lsum = lsum + pr
            acc = acc + pr * v_refs[i][h]
        l_sc[r:r + 1, :] = lsum
        acc_sc[r] = acc

    @pl.when(j == nj - 1)
    def _():
        lam = lam_ref[0:1, 0:1]
        for h in range(DA_HEADS):
            qk = q_ref[0, h] * kn_ref[0, h]
            ws = []
            for mp in range(2):
                r = 2 * h + mp
                s_new = jnp.sum(qk[32 * mp:32 * mp + 32], axis=0, keepdims=True) * scale
                m_prev = m_sc[r:r + 1, 0:1]
                m_fin = jnp.maximum(m_prev, s_new)
                alpha = jnp.exp(m_prev - m_fin)
                p_new = jnp.exp(s_new - m_fin)
                l_tot = alpha * jnp.sum(l_sc[r:r + 1, :], axis=1, keepdims=True) + p_new
                out = alpha * jnp.sum(acc_sc[r], axis=1, keepdims=True) + p_new * vn_ref[0, h]
                ws.append(out / l_tot)
            o = ws[0] - lam * ws[1]
            ms = jnp.mean(o * o, axis=0, keepdims=True)
            o_ref[0, h] = (o * lax.rsqrt(ms + RMS_EPS)) * dnw_ref[...] * (1.0 - lam_init)


def _attn_decode(q4, kn4, vn4, cache_kt, cache_vt, page_table, slopes8, lam, dnw_col, layer, lam_init):
    nb, n_pages = page_table.shape
    npg = _PAGES_PER_STEP
    past_len = n_pages * PAGE_SIZE

    def page_spec(i):
        return pl.BlockSpec((None, None, DA_HEADS, DA_DV, PAGE_SIZE), lambda b, j, pt: (layer, pt[b, j * npg + i], 0, 0, 0))

    col_spec = pl.BlockSpec((1, DA_HEADS, DA_DV, 1), lambda b, j, pt: (b, 0, 0, 0))
    grid_spec = pltpu.PrefetchScalarGridSpec(
        num_scalar_prefetch=1,
        grid=(nb, n_pages // npg),
        in_specs=[
            col_spec, col_spec, col_spec,
            pl.BlockSpec((8, LANES), lambda b, j, pt: (0, 0)),
            pl.BlockSpec((1, LANES), lambda b, j, pt: (0, 0)),
            pl.BlockSpec((DA_DV, 1), lambda b, j, pt: (0, 0)),
        ] + [page_spec(i) for i in range(npg)] + [page_spec(i) for i in range(npg)],
        out_specs=col_spec,
        scratch_shapes=[
            pltpu.VMEM((npg, 16, PAGE_SIZE), F32),
            pltpu.VMEM((16, PAGE_SIZE), F32),
            pltpu.VMEM((16, PAGE_SIZE), F32),
            pltpu.VMEM((2 * DA_HEADS, DA_DV, PAGE_SIZE), F32),
        ],
    )
    return pl.pallas_call(
        functools.partial(_attn_dec_body, past_len=past_len, lam_init=lam_init),
        grid_spec=grid_spec,
        out_shape=jax.ShapeDtypeStruct((nb, DA_HEADS, DA_DV, 1), F32),
        compiler_params=_cparams(("parallel", "arbitrary")),
        name="attn_decode",
    )(page_table, q4, kn4, vn4, slopes8, lam, dnw_col, *([cache_kt] * npg), *([cache_vt] * npg))


def _post_body(ys_ref, y5_ref, ya_ref, x_ref, wo_ref, wo_lo_ref, nw_ref, rhi_ref, rlo_ref, rb_ref, o_ref, *, hp):
    tm = x_ref.shape[0]
    mixed_in = jnp.concatenate([ys_ref[...], y5_ref[...], ya_ref[...]], axis=1)
    if hp:
        xm = x_ref[...] + _dot3(mixed_in, wo_ref[...], wo_lo_ref[...])
    else:
        xm = x_ref[...] + _dot(mixed_in.astype(BF16), wo_ref[...])
    t = _rms(xm, nw_ref[...])
    logits = _dot3(t, rhi_ref[...], rlo_ref[...]) + rb_ref[...]
    lane = lax.broadcasted_iota(jnp.int32, (tm, LANES), 1).astype(F32)
    big = 1e9
    gmask = lane < MOE_GROUPS
    gl = jnp.where(gmask, logits, -jnp.inf)
    gmax = jnp.max(gl, axis=1, keepdims=True)
    gidx = jnp.min(jnp.where(gl == gmax, lane, big), axis=1, keepdims=True)
    gsum = jnp.sum(jnp.where(gmask, jnp.exp(logits - gmax), 0.0), axis=1, keepdims=True)
    gprob = 1.0 / gsum
    e0 = 16.0 + gidx * EXPERTS_PER_GROUP
    emask = (lane >= e0) & (lane < e0 + EXPERTS_PER_GROUP)
    el = jnp.where(emask, logits, -jnp.inf)
    emax = jnp.max(el, axis=1, keepdims=True)
    i1 = jnp.min(jnp.where(el == emax, lane, big), axis=1, keepdims=True)
    el2 = jnp.where(emask & (lane != i1), logits, -jnp.inf)
    emax2 = jnp.max(el2, axis=1, keepdims=True)
    i2 = jnp.min(jnp.where(el2 == emax2, lane, big), axis=1, keepdims=True)
    p2 = jnp.exp(emax2 - emax)
    den = 1.0 + p2
    w1 = (1.0 / den) * gprob
    w2 = (p2 / den) * gprob
    a1 = i1 - e0
    a2 = i2 - e0
    first_low = a1 < a2
    lo = jnp.where(first_low, a1, a2)
    hi = jnp.where(first_low, a2, a1)
    w_lo = jnp.where(first_low, w1, w2)
    w_hi = jnp.where(first_low, w2, w1)
    pair = jnp.where(lo == 0.0, 0.0, jnp.where(lo == 1.0, 3.0, 5.0)) + (hi - lo - 1.0)
    cls = gidx * 6.0 + pair
    meta = jnp.where(lane == 0.0, w_lo, jnp.where(lane == 1.0, w_hi, jnp.where(lane == 2.0, cls, 0.0)))
    o_ref[:, 0:D_MODEL] = xm
    o_ref[:, D_MODEL:ROW_W] = meta


def _post(y_ssd, y_s5, y_da, x2d, wo, wo_lo, nw, rhi, rlo, rb, tm, hp):
    m = x2d.shape[0]
    const2 = lambda i: (0, 0)
    return pl.pallas_call(
        functools.partial(_post_body, hp=hp),
        grid=(m // tm,),
        in_specs=[
            pl.BlockSpec((tm, SSD_INNER), lambda i: (i, 0)),
            pl.BlockSpec((tm, S5_WIDTH), lambda i: (i, 0)),
            pl.BlockSpec((tm, DA_WIDTH), lambda i: (i, 0)),
            pl.BlockSpec((tm, D_MODEL), lambda i: (i, 0)),
            pl.BlockSpec((D_MODEL, D_MODEL), const2),
            pl.BlockSpec((D_MODEL, D_MODEL), const2),
            pl.BlockSpec((1, D_MODEL), const2),
            pl.BlockSpec((D_MODEL, LANES), const2),
            pl.BlockSpec((D_MODEL, LANES), const2),
            pl.BlockSpec((1, LANES), const2),
        ],
        out_specs=pl.BlockSpec((tm, ROW_W), lambda i: (i, 0)),
        out_shape=jax.ShapeDtypeStruct((m, ROW_W), F32),
        compiler_params=_cparams(("parallel",)),
        name="post",
    )(y_ssd, y_s5, y_da, x2d, wo, wo_lo, nw, rhi, rlo, rb)


def _row_copy(src, dst, sem):
    return pltpu.make_async_copy(src, dst, sem)


def _scatter_body(dest_ref, x_ref, init_ref, o_hbm, sem, *, ts):
    del init_ref
    base = pl.program_id(0) * ts

    def issue(r, c):
        d = dest_ref[base + r]
        _row_copy(x_ref.at[pl.ds(r, 1)], o_hbm.at[pl.ds(d, 1)], sem).start()
        return c

    lax.fori_loop(0, ts, issue, 0, unroll=8)
    _row_copy(x_ref, o_hbm.at[pl.ds(0, ts)], sem).wait()


def _moe_scatter(dest, rows, mp, ts):
    m = rows.shape[0]
    init = jnp.zeros((mp, ROW_W), F32)
    grid_spec = pltpu.PrefetchScalarGridSpec(
        num_scalar_prefetch=1,
        grid=(m // ts,),
        in_specs=[
            pl.BlockSpec((ts, ROW_W), lambda i, d: (i, 0)),
            pl.BlockSpec(memory_space=pl.ANY),
        ],
        out_specs=pl.BlockSpec(memory_space=pl.ANY),
        scratch_shapes=[pltpu.SemaphoreType.DMA(())],
    )
    return pl.pallas_call(
        functools.partial(_scatter_body, ts=ts),
        grid_spec=grid_spec,
        out_shape=jax.ShapeDtypeStruct((mp, ROW_W), F32),
        input_output_aliases={2: 0},
        compiler_params=_cparams(("arbitrary",)),
        name="moe_scatter",
    )(dest, rows, init)


def _moe_body(blk_ref, val_ref, e1_ref, e2_ref, x_ref, nw_ref, wg1_ref, wu1_ref, wd1_ref, wg2_ref, wu2_ref, wd2_ref, o_ref):
    del blk_ref, e1_ref, e2_ref
    t = pl.program_id(0)

    @pl.when(val_ref[t] > 0)
    def _():
        x = x_ref[:, 0:D_MODEL]
        w_lo = x_ref[:, D_MODEL:D_MODEL + 1]
        w_hi = x_ref[:, D_MODEL + 1:D_MODEL + 2]
        tb = _rms(x, nw_ref[...]).astype(BF16)

        def expert(wg, wu, wd):
            hid = jax.nn.silu(_dot(tb, wg[0])) * _dot(tb, wu[0])
            return _dot(hid.astype(BF16), wd[0])

        moe = w_lo * expert(wg1_ref, wu1_ref, wd1_ref) + w_hi * expert(wg2_ref, wu2_ref, wd2_ref)
        o_ref[...] = x + moe

    @pl.when(val_ref[t] == 0)
    def _():
        o_ref[...] = jnp.zeros(o_ref.shape, F32)


def _moe_experts(tile_blk, tile_valid, tile_e1, tile_e2, xs, nw, wg, wu, wd, tmo):
    mp = xs.shape[0]
    nt = mp // tmo
    up = lambda sel: pl.BlockSpec((1, D_MODEL, EXPERT_HIDDEN), (lambda t, tb, tv, e1, e2: ((e1, e2)[sel][t], 0, 0)))
    down = lambda sel: pl.BlockSpec((1, EXPERT_HIDDEN, D_MODEL), (lambda t, tb, tv, e1, e2: ((e1, e2)[sel][t], 0, 0)))
    grid_spec = pltpu.PrefetchScalarGridSpec(
        num_scalar_prefetch=4,
        grid=(nt,),
        in_specs=[
            pl.BlockSpec((tmo, ROW_W), lambda t, tb, tv, e1, e2: (tb[t], 0)),
            pl.BlockSpec((1, D_MODEL), lambda t, tb, tv, e1, e2: (0, 0)),
            up(0), up(0), down(0), up(1), up(1), down(1),
        ],
        out_specs=pl.BlockSpec((tmo, D_MODEL), lambda t, tb, tv, e1, e2: (t, 0)),
    )
    return pl.pallas_call(
        _moe_body,
        grid_spec=grid_spec,
        out_shape=jax.ShapeDtypeStruct((mp, D_MODEL), F32),
        compiler_params=_cparams(("arbitrary",)),
        name="moe_experts",
    )(tile_blk, tile_valid, tile_e1, tile_e2, xs, nw, wg, wu, wd, wg, wu, wd)


def _gather_body(dest_ref, ys_hbm, fnw_ref, o_ref, sem, *, ts, final_norm):
    base = pl.program_id(0) * ts

    def issue(r, c):
        d = dest_ref[base + r]
        _row_copy(ys_hbm.at[pl.ds(d, 1)], o_ref.at[pl.ds(r, 1)], sem).start()
        return c

    lax.fori_loop(0, ts, issue, 0, unroll=8)
    _row_copy(ys_hbm.at[pl.ds(0, ts)], o_ref, sem).wait()
    if final_norm:
        o_ref[...] = _rms(o_ref[...], fnw_ref[...])


def _moe_gather(dest, ys, fnw, m, ts, final_norm):
    grid_spec = pltpu.PrefetchScalarGridSpec(
        num_scalar_prefetch=1,
        grid=(m // ts,),
        in_specs=[
            pl.BlockSpec(memory_space=pl.ANY),
            pl.BlockSpec((1, D_MODEL), lambda i, d: (0, 0)),
        ],
        out_specs=pl.BlockSpec((ts, D_MODEL), lambda i, d: (i, 0)),
        scratch_shapes=[pltpu.SemaphoreType.DMA(())],
    )
    return pl.pallas_call(
        functools.partial(_gather_body, ts=ts, final_norm=final_norm),
        grid_spec=grid_spec,
        out_shape=jax.ShapeDtypeStruct((m, D_MODEL), F32),
        compiler_params=_cparams(("arbitrary",)),
        name="moe_gather",
    )(dest, ys, fnw)


def _moe_plan(cls, tmo):
    m = cls.shape[0]
    nt = m // tmo + N_CLASSES
    onehot = (cls[:, None] == jnp.arange(N_CLASSES, dtype=jnp.int32)[None, :]).astype(jnp.int32)
    csum = jnp.cumsum(onehot, axis=0)
    rank = jnp.sum(csum * onehot, axis=1) - 1
    counts = csum[-1]
    ntiles = (counts + tmo - 1) // tmo
    tile_end = jnp.cumsum(ntiles)
    tile_start = tile_end - ntiles
    dest = jnp.sum(onehot * (tile_start * tmo)[None, :], axis=1) + rank
    total = tile_end[-1]
    tid = jnp.arange(nt, dtype=jnp.int32)
    tile_valid = (tid < total).astype(jnp.int32)
    tile_blk = jnp.minimum(tid, total - 1)
    tile_cls = jnp.minimum(jnp.sum((tile_blk[:, None] >= tile_end[None, :]).astype(jnp.int32), axis=1), N_CLASSES - 1)
    grp = tile_cls // 6
    pair = tile_cls % 6
    lo = jnp.asarray(_PAIR_LO, jnp.int32)[pair]
    hi = jnp.asarray(_PAIR_HI, jnp.int32)[pair]
    return dest.astype(jnp.int32), tile_blk, tile_valid, grp * EXPERTS_PER_GROUP + lo, grp * EXPERTS_PER_GROUP + hi


def _moe(rows, nw2, wg, wu, wd, fnw, ts, tmo, final_norm):
    m = rows.shape[0]
    cls = rows[:, D_MODEL + 2].astype(jnp.int32)
    dest, tile_blk, tile_valid, e1, e2 = _moe_plan(cls, tmo)
    mp = m + N_CLASSES * tmo
    xs = _moe_scatter(dest, rows, mp, ts)
    ys = _moe_experts(tile_blk, tile_valid, e1, e2, xs, nw2, wg, wu, wd, tmo)
    return _moe_gather(dest, ys, fnw, m, ts, final_norm)


def _pad_lanes(v, width):
    return jnp.pad(v, ((0, 0), (0, width - v.shape[1])))


def _layer_params(l, p):
    w_in = p["w_in"][l]
    wp = jnp.concatenate([w_in[:, :OFF_DT], _pad_lanes(w_in[:, OFF_DT:OFF_U], LANES), w_in[:, OFF_U:]], axis=1)
    lp = {"norm1": p["norm1_w"][l][None, :]}
    lp["wp"], lp["wp_lo"] = _split2_param(wp)
    lp["cw8"] = jnp.pad(p["conv_w"][l], ((0, 4), (0, 0)))
    lp["cb"] = p["conv_b"][l][None, :]
    lp["dtb"] = _pad_lanes(p["dt_bias"][l][None, :], LANES)
    lp["alog"] = _pad_lanes(p["a_log"][l][None, :], LANES)
    lp["dvec"] = jnp.repeat(p["ssd_d"][l], SSD_HEAD_DIM)[None, :]
    lp["ssd_nw"] = p["ssd_norm_w"][l][None, :]
    lam = lax.complex(p["s5_lambda_re"][l], p["s5_lambda_im"][l])
    step = jnp.exp(p["s5_log_step"][l])[:, None]
    lam_bar = jnp.exp(lam * step)
    b = lax.complex(p["s5_b_re"][l], p["s5_b_im"][l])
    b_bar = ((lam_bar - 1.0) / lam)[..., None] * b
    eye = jnp.eye(S5_GROUPS, dtype=F32)

    def bdiag_in(m_gnc):
        return jnp.einsum("gnc,gh->gchn", m_gnc, eye).reshape(S5_WIDTH, S5_N)

    def bdiag_out(m_gcn):
        return jnp.einsum("gcn,gh->gnhc", m_gcn, eye).reshape(S5_N, S5_WIDTH)

    bmat = jnp.concatenate([bdiag_in(b_bar.real), bdiag_in(b_bar.imag)], axis=1)
    cmat = jnp.concatenate([bdiag_out(p["s5_c_re"][l]), -bdiag_out(p["s5_c_im"][l])], axis=0)
    lp["bhi"], lp["blo"] = _split2_param(bmat)
    lp["chi"], lp["clo"] = _split2_param(cmat)
    a_re = lam_bar.real.reshape(1, S5_N)
    a_im = lam_bar.imag.reshape(1, S5_N)
    lp["are"] = jnp.concatenate([a_re, a_re], axis=1)
    lp["aim"] = jnp.concatenate([-a_im, a_im], axis=1)
    lp["s5_d"] = p["s5_d"][l][None, :]
    lp["wglu"], lp["wglu_lo"] = _split2_param(p["s5_w_glu"][l])
    lp["bglu"] = p["s5_b_glu"][l][None, :]
    lam_init = 0.8 - 0.6 * math.exp(-0.3 * l)
    lam_s = (jnp.exp(jnp.sum(p["lam_q1"][l] * p["lam_k1"][l])) - jnp.exp(jnp.sum(p["lam_q2"][l] * p["lam_k2"][l])) + lam_init)
    lp["lam"] = jnp.full((1, LANES), lam_s, F32)
    lp["lam_init"] = lam_init
    lp["dnw_col"] = p["da_norm_w"][l][:, None]
    lp["wo"], lp["wo_lo"] = _split2_param(p["w_out"][l])
    lp["norm2"] = p["norm2_w"][l][None, :]
    wr = jnp.zeros((D_MODEL, LANES), F32)
    wr = wr.at[:, 0:MOE_GROUPS].set(p["w_router_group"][l]).at[:, 16:16 + N_EXPERTS].set(p["w_router_expert"][l])
    lp["rhi"], lp["rlo"] = _split2_param(wr)
    rb = jnp.zeros((1, LANES), F32)
    lp["rb"] = rb.at[0, 0:MOE_GROUPS].set(p["b_router_group"][l]).at[0, 16:16 + N_EXPERTS].set(p["b_router_expert"][l])
    lp["wg"] = p["w_gate"][l].astype(BF16)
    lp["wu"] = p["w_up"][l].astype(BF16)
    lp["wd"] = p["w_down"][l].astype(BF16)
    return lp


def _mixers(x2d, nb, t_len, lp, conv_buf, ssd_h0, s5_s0, attend, tm_proj, hp):
    z, xbc, dt, u, k, v, *qkv = _inproj(x2d, lp["norm1"], lp["wp"], lp["wp_lo"], tm_proj, hp)
    L = SSD_CHUNK
    tp = -(-t_len // L) * L

    def seq(a):
        a = a.reshape(nb, t_len, a.shape[-1])
        return a if tp == t_len else jnp.pad(a, ((0, 0), (0, tp - t_len), (0, 0)))

    cbuf0 = jnp.pad(conv_buf, ((0, 0), (5, 0), (0, 0)))
    tri = jnp.tril(jnp.ones((L, L), F32)).astype(BF16)
    y_ssd, cnew, ssd_new = _ssd(seq(xbc), seq(z), seq(dt), cbuf0, ssd_h0, lp["cw8"], lp["cb"], lp["dtb"], lp["alog"],
                             lp["dvec"], lp["ssd_nw"], tri, t_len, hp)
    y_ssd = y_ssd[:, :t_len].reshape(nb * t_len, SSD_INNER)
    conv_new = cnew[:, 0:3]

    u_tb = jnp.swapaxes(u.reshape(nb, t_len, S5_WIDTH), 0, 1).reshape(t_len * nb, S5_WIDTH)
    s0 = jnp.concatenate([s5_s0[..., 0].reshape(nb, S5_N), s5_s0[..., 1].reshape(nb, S5_N)], axis=1)
    tc = min(t_len, 64)
    y5_tb, sfin = _s5(u_tb, s0, lp["are"], lp["aim"], lp["bhi"], lp["blo"], lp["chi"], lp["clo"], lp["s5_d"],
                      lp["wglu"], lp["wglu_lo"], lp["bglu"], nb, tc, hp)
    y_s5 = jnp.swapaxes(y5_tb.reshape(t_len, nb, S5_WIDTH), 0, 1).reshape(nb * t_len, S5_WIDTH)
    s5_new = jnp.stack([sfin[:, :S5_N].reshape(nb, S5_GROUPS, S5_STATE), sfin[:, S5_N:].reshape(nb, S5_GROUPS, S5_STATE)],
                       axis=-1)

    y_da = attend(k, v, *qkv)
    k_rows = k.reshape(nb, t_len, DA_HEADS, 2 * DA_DQK)
    v_rows = v.reshape(nb, t_len, DA_HEADS, DA_DV)
    return y_ssd, y_s5, y_da, (k_rows, v_rows, conv_new, ssd_new, s5_new)


def _trunk_layer(x2d, nb, t_len, lp, conv_buf, ssd_h0, s5_s0, attend, fnw, final_norm, tm_proj, ts, tmo, hp):
    y_ssd, y_s5, y_da, states = _mixers(x2d, nb, t_len, lp, conv_buf, ssd_h0, s5_s0, attend, tm_proj, hp)
    rows = _post(y_ssd, y_s5, y_da, x2d, lp["wo"], lp["wo_lo"], lp["norm2"], lp["rhi"], lp["rlo"], lp["rb"], tm_proj, hp)
    x_new = _moe(rows, lp["norm2"], lp["wg"], lp["wu"], lp["wd"], fnw, ts, tmo, final_norm)
    return x_new, states


def kernel(x_prompt, x_sample, cache_k, cache_v, page_table, state_conv, state_ssd, state_s5, norm1_w, w_in, conv_w, conv_b, dt_bias, a_log, ssd_d, ssd_norm_w, s5_lambda_re, s5_lambda_im, s5_log_step, s5_b_re, s5_b_im, s5_c_re, s5_c_im, s5_d, s5_w_glu, s5_b_glu, lam_q1, lam_k1, lam_q2, lam_k2, da_norm_w, w_out, norm2_w, w_router_group, b_router_group, w_router_expert, b_router_expert, w_gate, w_up, w_down, final_norm_w):
    params = dict(
        norm1_w=norm1_w, w_in=w_in, conv_w=conv_w, conv_b=conv_b, dt_bias=dt_bias, a_log=a_log, ssd_d=ssd_d,
        ssd_norm_w=ssd_norm_w, s5_lambda_re=s5_lambda_re, s5_lambda_im=s5_lambda_im, s5_log_step=s5_log_step,
        s5_b_re=s5_b_re, s5_b_im=s5_b_im, s5_c_re=s5_c_re, s5_c_im=s5_c_im, s5_d=s5_d, s5_w_glu=s5_w_glu,
        s5_b_glu=s5_b_glu, lam_q1=lam_q1, lam_k1=lam_k1, lam_q2=lam_q2, lam_k2=lam_k2, da_norm_w=da_norm_w,
        w_out=w_out, norm2_w=norm2_w, w_router_group=w_router_group, b_router_group=b_router_group,
        w_router_expert=w_router_expert, b_router_expert=b_router_expert, w_gate=w_gate, w_up=w_up, w_down=w_down)
    bp, t_p, _ = x_prompt.shape
    bs, t_s, _ = x_sample.shape
    slopes = jnp.exp2(-8.0 * jnp.arange(1, DA_HEADS + 1, dtype=F32) / DA_HEADS)
    slopes8 = jnp.broadcast_to(jnp.pad(slopes, (0, 2))[:, None], (8, LANES))
    cache_kt = jnp.transpose(cache_k, (0, 1, 3, 4, 2))
    cache_vt = jnp.transpose(cache_v, (0, 1, 3, 4, 2))
    fnw = final_norm_w[None, :]

    xp = x_prompt.reshape(bp * t_p, D_MODEL)
    xs = x_sample.reshape(bs * t_s, D_MODEL)
    p_states, s_states = [], []
    for l in range(DEPTH):
        lp = _layer_params(l, params)
        last = l == DEPTH - 1

        hp_prompt = l == 0

        def attend_p(k, v, q, kb=None, vb=None, lp=lp, hp=hp_prompt):
            shp = (bp, t_p, DA_WIDTH)
            kk, vv = (k, v) if hp else (kb, vb)
            return _attn_prompt(q.reshape(shp), kk.reshape(shp), vv.reshape(shp), slopes8, lp["lam"], lp["dnw_col"],
                                lp["lam_init"], 512, hp).reshape(bp * t_p, DA_WIDTH)

        def attend_s(k, v, q, lp=lp, l=l):
            shp = (bs, DA_HEADS, DA_DV, 1)
            return _attn_decode(q.reshape(shp), k.reshape(shp), v.reshape(shp), cache_kt, cache_vt, page_table, slopes8,
                                lp["lam"], lp["dnw_col"], l, lp["lam_init"]).reshape(bs, DA_WIDTH)

        xp, st_p = _trunk_layer(
            xp, bp, t_p, lp,
            jnp.zeros((bp, SSD_CONV - 1, SSD_XBC), F32),
            jnp.zeros((bp, SSD_HEADS, SSD_HEAD_DIM, SSD_STATE), F32),
            jnp.zeros((bp, S5_GROUPS, S5_STATE, 2), F32),
            attend_p, fnw, last, 512, 256, 256, hp_prompt)
        p_states.append(st_p)
        xs, st_s = _trunk_layer(xs, bs, t_s, lp, state_conv[l], state_ssd[l], state_s5[l], attend_s, fnw, last, 32, 32, 16, True)
        s_states.append(st_s)
    k_p, v_p, conv_p, ssd_p, s5_p = [jnp.stack([st[i] for st in p_states]) for i in range(5)]
    k_s, v_s, conv_s, ssd_s, s5_s = [jnp.stack([st[i] for st in s_states]) for i in range(5)]
    y_prompt = xp.reshape(bp, t_p, D_MODEL)
    y_sample = xs.reshape(bs, t_s, D_MODEL)
    return (y_prompt, y_sample, k_p, v_p, k_s, v_s, conv_p, conv_s, ssd_p, ssd_s, s5_p, s5_s)
```

```python
import functools
import math

import jax
import jax.numpy as jnp
import numpy as np
from jax import lax
from jax.experimental import pallas as pl
from jax.experimental.pallas import tpu as pltpu

F32 = jnp.float32
BF16 = jnp.bfloat16

D_MODEL = 1024
DEPTH = 2
PAGE_SIZE = 128
SSD_HEAD_DIM = 64
SSD_INNER = 384
SSD_HEADS = 6
SSD_GROUPS = 2
SSD_STATE = 64
SSD_CONV = 4
SSD_XBC = 640
SSD_CHUNK = 128
DA_DQK = 32
DA_DV = 64
DA_WIDTH = 384
DA_HEADS = 6
S5_WIDTH = 256
S5_GROUP_CH = 16
S5_GROUPS = 16
S5_STATE = 64
S5_N = S5_GROUPS * S5_STATE
OFF_XBC = 384
OFF_DT = 1024
OFF_U = 1030
OFF_Q = 1286
OFF_K = 1670
OFF_V = 2054
N_IN = 2438
N_IN_PAD = 2560
MOE_GROUPS = 4
EXPERTS_PER_GROUP = 4
N_EXPERTS = 16
N_CLASSES = 24
EXPERT_HIDDEN = 512
RMS_EPS = 1e-6
LANES = 128
META_W = 128
ROW_W = D_MODEL + META_W
VMEM_LIMIT = 56 * 1024 * 1024
NEG_BIG = -1e30
LOG2E = 1.4426950408889634

_PAIR_LO = (0, 0, 0, 1, 1, 2)
_PAIR_HI = (1, 2, 3, 2, 3, 3)


def _cparams(sem):
    return pltpu.CompilerParams(dimension_semantics=sem, vmem_limit_bytes=VMEM_LIMIT)


def _rms(x, w):
    ms = jnp.mean(x * x, axis=-1, keepdims=True)
    return (x * lax.rsqrt(ms + RMS_EPS)) * w


def _softplus(x):
    return jnp.maximum(x, 0.0) + jnp.log(1.0 + jnp.exp(-jnp.abs(x)))


def _split2(a):
    hi = a.astype(BF16)
    lo = (a - hi.astype(F32)).astype(BF16)
    return hi, lo


def _split2_param(a):
    hi = lax.reduce_precision(a, exponent_bits=8, mantissa_bits=7)
    return hi.astype(BF16), (a - hi).astype(BF16)


def _dot(a, b):
    return jnp.dot(a, b, preferred_element_type=F32)


def _dot_nt(a, b):
    return lax.dot_general(a, b, (((1,), (1,)), ((), ())), preferred_element_type=F32)


def _dot_tn(a, b):
    return lax.dot_general(a, b, (((0,), (0,)), ((), ())), preferred_element_type=F32)


def _mm(a, b, hp, nt=False):
    d = _dot_nt if nt else _dot
    if not hp:
        return d(a, b)
    a_hi, a_lo = _split2(a)
    b_hi, b_lo = _split2(b)
    return d(a_hi, b_hi) + (d(a_lo, b_hi) + d(a_hi, b_lo))


def _dot3(a, b_hi, b_lo):
    a_hi, a_lo = _split2(a)
    return _dot(a_hi, b_hi) + (_dot(a_lo, b_hi) + _dot(a_hi, b_lo))


def _inproj_body(x_ref, nw_ref, *refs, hp):
    h = _rms(x_ref[...], nw_ref[...])
    if hp:
        whi_ref, wlo_ref = refs[:2]
        outs = refs[2:]
        p = _dot3(h, whi_ref[...], wlo_ref[...])
    else:
        outs = refs[1:]
        p = _dot(h.astype(BF16), refs[0][...])
    z_ref, xbc_ref, dt_ref, u_ref, k_ref, v_ref = outs[:6]
    z_ref[...] = p[:, 0:384]
    xbc_ref[...] = p[:, 384:1024]
    dt_ref[...] = p[:, 1024:1152]
    u_ref[...] = p[:, 1152:1408]
    q = p[:, 1408:1792]
    k = p[:, 1792:2176]
    v = p[:, 2176:2560]
    k_ref[...] = k
    v_ref[...] = v
    bf = outs[6:]
    if hp:
        outs[6][...] = q
        bf = outs[7:]
    bf[0][...] = q.astype(BF16)
    bf[1][...] = k.astype(BF16)
    bf[2][...] = v.astype(BF16)


def _inproj(x2d, nw, w_hi, w_lo, tm, hp):
    m = x2d.shape[0]
    widths = (384, 640, 128, 256, 384, 384) + ((384,) if hp else ()) + (384, 384, 384)
    dtypes = (F32,) * 6 + ((F32,) if hp else ()) + (BF16, BF16, BF16)
    w_spec = pl.BlockSpec((D_MODEL, N_IN_PAD), lambda i: (0, 0))
    weights = (w_hi, w_lo) if hp else (w_hi,)
    return pl.pallas_call(
        functools.partial(_inproj_body, hp=hp),
        grid=(m // tm,),
        in_specs=[
            pl.BlockSpec((tm, D_MODEL), lambda i: (i, 0)),
            pl.BlockSpec((1, D_MODEL), lambda i: (0, 0)),
        ] + [w_spec] * len(weights),
        out_specs=[pl.BlockSpec((tm, w), lambda i: (i, 0)) for w in widths],
        out_shape=[jax.ShapeDtypeStruct((m, w), dt) for w, dt in zip(widths, dtypes)],
        compiler_params=_cparams(("parallel",)),
        name="inproj",
    )(x2d, nw, *weights)


def _ssd_body(xbc_ref, z_ref, dt_ref, cbuf0_ref, h0_ref, cw_ref, cb_ref, dtb_ref, alog_ref, dvec_ref, nw_ref, tri_ref,
              y_ref, cnew_ref, hfin_ref, cbuf, hs, *, t_valid, hp):
    L = SSD_CHUNK
    c = pl.program_id(1)
    nc = pl.num_programs(1)

    @pl.when(c == 0)
    def _():
        cbuf[0:8, :] = cbuf0_ref[0]
        hs[...] = jnp.zeros(hs.shape, F32)
        for h in range(SSD_HEADS):
            g = h // 3
            hs[h * 64:(h + 1) * 64, g * 64:(g + 1) * 64] = h0_ref[0, h]
        cnew_ref[0] = jnp.zeros((8, SSD_XBC), F32)

    cbuf[8:8 + L, :] = xbc_ref[0]
    acc = cb_ref[...] + cbuf[5:5 + L, :] * cw_ref[0:1, :]
    acc = acc + cbuf[6:6 + L, :] * cw_ref[1:2, :]
    acc = acc + cbuf[7:7 + L, :] * cw_ref[2:3, :]
    acc = acc + cbuf[8:8 + L, :] * cw_ref[3:4, :]
    act = jax.nn.silu(acc)

    c_last = (t_valid - 1) // L
    r_last = (t_valid - 1) % L

    @pl.when(c == c_last)
    def _():
        cnew_ref[0, 0:3, :] = cbuf[6 + r_last:9 + r_last, :]

    tail = cbuf[L:L + 8, :]
    cbuf[0:8, :] = tail

    xs = act[:, 0:384]
    bm = act[:, 384:512]
    cm = act[:, 512:640]

    rows = lax.broadcasted_iota(jnp.int32, (L, LANES), 0) + c * L
    dtv = _softplus(dt_ref[0] + dtb_ref[...])
    dtv = jnp.where(rows < t_valid, dtv, 0.0)
    a = -jnp.exp(alog_ref[...])
    dta = dtv * a
    tri = tri_ref[...]
    d_hi = dta.astype(BF16)
    r1 = dta - d_hi.astype(F32)
    d_mid = r1.astype(BF16)
    d_lo = (r1 - d_mid.astype(F32)).astype(BF16)
    acum = _dot(tri, d_hi) + (_dot(tri, d_mid) + _dot(tri, d_lo))
    acum_t = acum.T
    dt_t = dtv.T
    xs_t = xs.T

    lane = lax.broadcasted_iota(jnp.int32, (L, LANES), 1)
    low = lane < 64
    li = lax.broadcasted_iota(jnp.int32, (L, L), 0)
    si = lax.broadcasted_iota(jnp.int32, (L, L), 1)
    causal = li >= si
    pre = (lambda a: a) if hp else (lambda a: a.astype(BF16))
    bm_b = pre(bm)
    cm_b = pre(cm)
    xs_b = pre(xs)
    xs_tb = pre(xs_t)
    cb_g = [
        _mm(pre(jnp.where(low, cm, 0.0)), bm_b, hp, nt=True),
        _mm(pre(jnp.where(low, 0.0, cm)), bm_b, hp, nt=True),
    ]
    lane_h = lax.broadcasted_iota(jnp.int32, (64, LANES), 1)

    y_pairs = []
    for p in range(3):
        halves = []
        ecols = []
        for half in range(2):
            h = 2 * p + half
            g = h // 3
            col = acum[:, h:h + 1]
            row = acum_t[h:h + 1, :]
            decay = jnp.where(causal, jnp.exp(col - row), 0.0)
            mh = (cb_g[g] * decay) * dt_t[h:h + 1, :]
            halves.append(_mm(pre(mh), xs_b[:, p * 128:(p + 1) * 128], hp))
            ecols.append(jnp.exp(col))
        y_diag = jnp.where(low, halves[0], halves[1])
        h_in = hs[p * 128:(p + 1) * 128, :]
        y_off = _mm(cm_b, pre(h_in), hp, nt=True) * jnp.where(low, ecols[0], ecols[1])
        y_pairs.append(y_diag + y_off)
        for half in range(2):
            h = 2 * p + half
            g = h // 3
            col = acum[:, h:h + 1]
            last = acum[L - 1:L, h:h + 1]
            w = jnp.exp(last - col) * dtv[:, h:h + 1]
            st = _mm(xs_tb[h * 64:(h + 1) * 64, :], pre(bm * w), hp)
            gmask = (lane_h < 64) if g == 0 else (lane_h >= 64)
            st = jnp.where(gmask, st, 0.0)
            hs[h * 64:(h + 1) * 64, :] = jnp.exp(last) * hs[h * 64:(h + 1) * 64, :] + st

    y = jnp.concatenate(y_pairs, axis=1) + dvec_ref[...] * xs
    y = y * jax.nn.silu(z_ref[0])
    y_ref[0] = _rms(y, nw_ref[...])

    @pl.when(c == nc - 1)
    def _():
        for h in range(SSD_HEADS):
            g = h // 3
            hfin_ref[0, h] = hs[h * 64:(h + 1) * 64, g * 64:(g + 1) * 64]


def _ssd(xbc, z, dt, cbuf0, h0, cw8, cb, dtb, alog, dvec, nw, tri, t_valid, hp):
    b, tp, _ = xbc.shape
    L = SSD_CHUNK
    const2 = lambda i, j: (0, 0)
    return pl.pallas_call(
        functools.partial(_ssd_body, t_valid=t_valid, hp=hp),
        grid=(b, tp // L),
        in_specs=[
            pl.BlockSpec((1, L, SSD_XBC), lambda i, j: (i, j, 0)),
            pl.BlockSpec((1, L, SSD_INNER), lambda i, j: (i, j, 0)),
            pl.BlockSpec((1, L, LANES), lambda i, j: (i, j, 0)),
            pl.BlockSpec((1, 8, SSD_XBC), lambda i, j: (i, 0, 0)),
            pl.BlockSpec((1, SSD_HEADS, SSD_HEAD_DIM, SSD_STATE), lambda i, j: (i, 0, 0, 0)),
            pl.BlockSpec((8, SSD_XBC), const2),
            pl.BlockSpec((1, SSD_XBC), const2),
            pl.BlockSpec((1, LANES), const2),
            pl.BlockSpec((1, LANES), const2),
            pl.BlockSpec((1, SSD_INNER), const2),
            pl.BlockSpec((1, SSD_INNER), const2),
            pl.BlockSpec((L, L), const2),
        ],
        out_specs=[
            pl.BlockSpec((1, L, SSD_INNER), lambda i, j: (i, j, 0)),
            pl.BlockSpec((1, 8, SSD_XBC), lambda i, j: (i, 0, 0)),
            pl.BlockSpec((1, SSD_HEADS, SSD_HEAD_DIM, SSD_STATE), lambda i, j: (i, 0, 0, 0)),
        ],
        out_shape=[
            jax.ShapeDtypeStruct((b, tp, SSD_INNER), F32),
            jax.ShapeDtypeStruct((b, 8, SSD_XBC), F32),
            jax.ShapeDtypeStruct((b, SSD_HEADS, SSD_HEAD_DIM, SSD_STATE), F32),
        ],
        scratch_shapes=[pltpu.VMEM((L + 8, SSD_XBC), F32), pltpu.VMEM((SSD_INNER, LANES), F32)],
        compiler_params=_cparams(("parallel", "arbitrary")),
        name="ssd",
    )(xbc, z, dt, cbuf0, h0, cw8, cb, dtb, alog, dvec, nw, tri)


def _s5_body(u_ref, s0_ref, are_ref, aim_ref, bhi_ref, blo_ref, chi_ref, clo_ref, dvec_ref, wglu_ref, wglu_lo_ref, bglu_ref,
             y_ref, sfin_ref, st, carry, *, nb, tc, hp):
    i = pl.program_id(0)

    @pl.when(i == 0)
    def _():
        carry[...] = s0_ref[...]

    u = u_ref[...]
    if hp:
        st[...] = _dot3(u, bhi_ref[...], blo_ref[...])
    else:
        st[...] = _dot(u.astype(BF16), bhi_ref[...])
    are = jnp.broadcast_to(are_ref[...], (nb, 2 * S5_N))
    aim = jnp.broadcast_to(aim_ref[...], (nb, 2 * S5_N))

    def step(t, s):
        r = pl.multiple_of(t * nb, nb)
        sw = jnp.concatenate([s[:, S5_N:], s[:, :S5_N]], axis=1)
        s = are * s + aim * sw + st[pl.ds(r, nb), :]
        st[pl.ds(r, nb), :] = s
        return s

    s_last = lax.fori_loop(0, tc, step, carry[...])
    carry[...] = s_last
    sfin_ref[...] = s_last

    if hp:
        y = _dot3(st[...], chi_ref[...], clo_ref[...]) + dvec_ref[...] * u
    else:
        y = _dot(st[...].astype(BF16), chi_ref[...]) + dvec_ref[...] * u
    y = jax.nn.gelu(y)
    if hp:
        g = _dot3(y, wglu_ref[...], wglu_lo_ref[...]) + bglu_ref[...]
    else:
        g = _dot(y.astype(BF16), wglu_ref[...]) + bglu_ref[...]
    y_ref[...] = g[:, :S5_WIDTH] * jax.nn.sigmoid(g[:, S5_WIDTH:])


def _s5(u_tb, s0, are, aim, bhi, blo, chi, clo, dvec, wglu, wglu_lo, bglu, nb, tc, hp):
    rows = u_tb.shape[0]
    blk = tc * nb
    const2 = lambda i: (0, 0)
    return pl.pallas_call(
        functools.partial(_s5_body, nb=nb, tc=tc, hp=hp),
        grid=(rows // blk,),
        in_specs=[
            pl.BlockSpec((blk, S5_WIDTH), lambda i: (i, 0)),
            pl.BlockSpec((nb, 2 * S5_N), const2),
            pl.BlockSpec((1, 2 * S5_N), const2),
            pl.BlockSpec((1, 2 * S5_N), const2),
            pl.BlockSpec((S5_WIDTH, 2 * S5_N), const2),
            pl.BlockSpec((S5_WIDTH, 2 * S5_N), const2),
            pl.BlockSpec((2 * S5_N, S5_WIDTH), const2),
            pl.BlockSpec((2 * S5_N, S5_WIDTH), const2),
            pl.BlockSpec((1, S5_WIDTH), const2),
            pl.BlockSpec((S5_WIDTH, 2 * S5_WIDTH), const2),
            pl.BlockSpec((S5_WIDTH, 2 * S5_WIDTH), const2),
            pl.BlockSpec((1, 2 * S5_WIDTH), const2),
        ],
        out_specs=[
            pl.BlockSpec((blk, S5_WIDTH), lambda i: (i, 0)),
            pl.BlockSpec((nb, 2 * S5_N), const2),
        ],
        out_shape=[
            jax.ShapeDtypeStruct((rows, S5_WIDTH), F32),
            jax.ShapeDtypeStruct((nb, 2 * S5_N), F32),
        ],
        scratch_shapes=[pltpu.VMEM((blk, 2 * S5_N), F32), pltpu.VMEM((nb, 2 * S5_N), F32)],
        compiler_params=_cparams(("arbitrary",)),
        name="s5",
    )(u_tb, s0, are, aim, bhi, blo, chi, clo, dvec, wglu, wglu_lo, bglu)


def _attn_body(qi_ref, kj_ref, q_ref, k_ref, v_ref, *refs, tq, nq, lam_init, hp):
    if hp:
        qf_ref, kf_ref, vf_ref = refs[:3]
        refs = refs[3:]
    slope_ref, lam_ref, dnw_ref, o_ref, m_sc, l_sc, acc_sc, bias_sc = refs
    p = pl.program_id(1)
    s = pl.program_id(2)
    qi = qi_ref[s]
    kj = kj_ref[s]
    c_qk = (DA_DQK ** -0.5) * LOG2E

    @pl.when(s == 0)
    def _():
        ki = lax.broadcasted_iota(jnp.int32, (tq, tq), 0)
        ql = lax.broadcasted_iota(jnp.int32, (tq, tq), 1)
        rel = (ql - ki).astype(F32)
        for half in range(2):
            bias_sc[half] = (slope_ref[pl.ds(2 * p + half, 1), 0:1] * (-LOG2E)) * rel

    @pl.when(kj == 0)
    def _():
        m_sc[...] = jnp.full(m_sc.shape, NEG_BIG, F32)
        l_sc[...] = jnp.zeros(l_sc.shape, F32)
        acc_sc[...] = jnp.zeros(acc_sc.shape, F32)

    def block(diag, hp):
        q = (qf_ref if hp else q_ref)[0]
        k = (kf_ref if hp else k_ref)[0]
        v = (vf_ref if hp else v_ref)[0]
        qlane = lax.broadcasted_iota(jnp.int32, (tq, LANES), 1)
        zero_q = jnp.zeros_like(q)
        if hp:
            k_hi, k_lo = _split2(k)
            v_hi, v_lo = _split2(v)
            k_cat = jnp.concatenate([k_hi, k_lo, k_hi], axis=1)
        if diag:
            causal = lax.broadcasted_iota(jnp.int32, (tq, tq), 0) <= lax.broadcasted_iota(jnp.int32, (tq, tq), 1)
        off = ((qi - kj) * tq).astype(F32)
        for half in range(2):
            cblk = (slope_ref[pl.ds(2 * p + half, 1), 0:1] * (-LOG2E)) * off
            for mp in range(2):
                cidx = half * 2 + mp
                lo = half * 64 + mp * 32
                qm = jnp.where((qlane >= lo) & (qlane < lo + 32), q, zero_q)
                if hp:
                    qm_hi, qm_lo = _split2(qm)
                    raw = _dot_nt(k_cat, jnp.concatenate([qm_hi, qm_hi, qm_lo], axis=1))
                else:
                    raw = _dot_nt(k, qm)
                sc = raw * c_qk + bias_sc[half]
                if diag:
                    sc = jnp.where(causal, sc, -jnp.inf)
                m_prev = m_sc[cidx:cidx + 1, :]
                m_new = jnp.maximum(m_prev, jnp.max(sc, axis=0, keepdims=True) + cblk)
                alpha = jnp.exp2(m_prev - m_new)
                pr = jnp.exp2(sc - (m_new - cblk))
                l_sc[cidx:cidx + 1, :] = alpha * l_sc[cidx:cidx + 1, :] + jnp.sum(pr, axis=0, keepdims=True)
                m_sc[cidx:cidx + 1, :] = m_new
                if hp:
                    p_hi, p_lo = _split2(pr)
                    pv = _dot_tn(v_hi, p_hi) + (_dot_tn(v_lo, p_hi) + _dot_tn(v_hi, p_lo))
                else:
                    pv = _dot_tn(v, pr.astype(BF16))
                r0 = half * 64
                acc_sc[mp, r0:r0 + 64, :] = alpha * acc_sc[mp, r0:r0 + 64, :] + pv[r0:r0 + 64, :]

    precise = (qi == nq - 1) if hp else False
    coarse = (qi != nq - 1) if hp else True

    @pl.when((kj != qi) & coarse)
    def _():
        block(False, False)

    @pl.when((kj == qi) & coarse)
    def _():
        block(True, False)

    if hp:
        @pl.when((kj != qi) & precise)
        def _():
            block(False, True)

        @pl.when((kj == qi) & precise)
        def _():
            block(True, True)

    @pl.when(kj == qi)
    def _():
        lam = lam_ref[0:1, 0:1]
        outs = []
        for half in range(2):
            r0 = half * 64
            o0 = acc_sc[0, r0:r0 + 64, :] / l_sc[half * 2:half * 2 + 1, :]
            o1 = acc_sc[1, r0:r0 + 64, :] / l_sc[half * 2 + 1:half * 2 + 2, :]
            o = o0 - lam * o1
            ms = jnp.mean(o * o, axis=0, keepdims=True)
            outs.append((o * lax.rsqrt(ms + RMS_EPS)) * dnw_ref[...] * (1.0 - lam_init))
        o_ref[0] = jnp.concatenate(outs, axis=0).T


def _attn_prompt(qb, kb, vb, qf, kf, vf, slopes, lam, dnw, lam_init, tq, hp):
    b, t, _ = qb.shape
    nq = t // tq
    qi_tab = np.array([i for i in range(nq) for _ in range(i + 1)], np.int32)
    kj_tab = np.array([j for i in range(nq) for j in range(i + 1)], np.int32)
    const2 = lambda i, p, s, qt, kt: (0, 0)
    in_specs = [
        pl.BlockSpec((1, tq, LANES), lambda i, p, s, qt, kt: (i, qt[s], p)),
        pl.BlockSpec((1, tq, LANES), lambda i, p, s, qt, kt: (i, kt[s], p)),
        pl.BlockSpec((1, tq, LANES), lambda i, p, s, qt, kt: (i, kt[s], p)),
    ]
    operands = [qb, kb, vb]
    if hp:
        kv_last = lambda i, p, s, qt, kt: (i, jnp.where(qt[s] == nq - 1, kt[s], 0), p)
        in_specs += [
            pl.BlockSpec((1, tq, LANES), lambda i, p, s, qt, kt: (i, nq - 1, p)),
            pl.BlockSpec((1, tq, LANES), kv_last),
            pl.BlockSpec((1, tq, LANES), kv_last),
        ]
        operands += [qf, kf, vf]
    in_specs += [pl.BlockSpec((8, LANES), const2), pl.BlockSpec((1, LANES), const2), pl.BlockSpec((64, 1), const2)]
    grid_spec = pltpu.PrefetchScalarGridSpec(
        num_scalar_prefetch=2,
        grid=(b, 3, len(qi_tab)),
        in_specs=in_specs,
        out_specs=pl.BlockSpec((1, tq, LANES), lambda i, p, s, qt, kt: (i, qt[s], p)),
        scratch_shapes=[pltpu.VMEM((4, tq), F32), pltpu.VMEM((4, tq), F32), pltpu.VMEM((2, LANES, tq), F32),
                        pltpu.VMEM((2, tq, tq), F32)],
    )
    return pl.pallas_call(
        functools.partial(_attn_body, tq=tq, nq=nq, lam_init=lam_init, hp=hp),
        grid_spec=grid_spec,
        out_shape=jax.ShapeDtypeStruct((b, t, DA_WIDTH), F32),
        compiler_params=_cparams(("parallel", "parallel", "arbitrary")),
        name="attn_prompt",
    )(jnp.asarray(qi_tab), jnp.asarray(kj_tab), *operands, slopes, lam, dnw)


_PAGES_PER_STEP = 16


def _attn_dec_body(pt_ref, q_ref, kn_ref, vn_ref, slope_ref, lam_ref, dnw_ref, *refs, past_len, lam_init):
    del pt_ref
    npg = _PAGES_PER_STEP
    k_refs = refs[:npg]
    v_refs = refs[npg:2 * npg]
    o_ref, s_sc, m_sc, l_sc, acc_sc = refs[2 * npg:]
    j = pl.program_id(1)
    nj = pl.num_programs(1)
    nrow = 2 * DA_HEADS

    @pl.when(j == 0)
    def _():
        m_sc[...] = jnp.full(m_sc.shape, NEG_BIG, F32)
        l_sc[...] = jnp.zeros(l_sc.shape, F32)
        acc_sc[...] = jnp.zeros(acc_sc.shape, F32)

    scale = DA_DQK ** -0.5
    lane_f = lax.broadcasted_iota(jnp.int32, (1, PAGE_SIZE), 1).astype(F32)
    mx = [None] * nrow
    for i in range(npg):
        dist = (past_len - (j * npg + i) * PAGE_SIZE).astype(F32) - lane_f
        for h in range(DA_HEADS):
            prod = k_refs[i][h] * q_ref[0, h]
            bias = slope_ref[h:h + 1, :] * dist
            for mp in range(2):
                r = 2 * h + mp
                sc = jnp.sum(prod[32 * mp:32 * mp + 32], axis=0, keepdims=True) * scale - bias
                s_sc[i, r:r + 1, :] = sc
                mx[r] = sc if mx[r] is None else jnp.maximum(mx[r], sc)
    for r in range(nrow):
        h = r // 2
        m_prev = m_sc[r:r + 1, :]
        m_new = jnp.maximum(m_prev, jnp.max(mx[r], axis=1, keepdims=True))
        alpha = jnp.exp(m_prev - m_new)
        m_sc[r:r + 1, :] = m_new
        lsum = alpha * l_sc[r:r + 1, :]
        acc = alpha * acc_sc[r]
        for i in range(npg):
            pr = jnp.exp(s_sc[i, r:r + 1, :] - m_new)
            lsum = lsum + pr
            acc = acc + pr * v_refs[i][h]
        l_sc[r:r + 1, :] = lsum
        acc_sc[r] = acc

    @pl.when(j == nj - 1)
    def _():
        lam = lam_ref[0:1, 0:1]
        for h in range(DA_HEADS):
            qk = q_ref[0, h] * kn_ref[0, h]
            ws = []
            for mp in range(2):
                r = 2 * h + mp
                s_new = jnp.sum(qk[32 * mp:32 * mp + 32], axis=0, keepdims=True) * scale
                m_prev = m_sc[r:r + 1, 0:1]
                m_fin = jnp.maximum(m_prev, s_new)
                alpha = jnp.exp(m_prev - m_fin)
                p_new = jnp.exp(s_new - m_fin)
                l_tot = alpha * jnp.sum(l_sc[r:r + 1, :], axis=1, keepdims=True) + p_new
                out = alpha * jnp.sum(acc_sc[r], axis=1, keepdims=True) + p_new * vn_ref[0, h]
                ws.append(out / l_tot)
            o = ws[0] - lam * ws[1]
            ms = jnp.mean(o * o, axis=0, keepdims=True)
            o_ref[0, h] = (o * lax.rsqrt(ms + RMS_EPS)) * dnw_ref[...] * (1.0 - lam_init)


def _attn_decode(q4, kn4, vn4, cache_kt, cache_vt, page_table, slopes8, lam, dnw_col, layer, lam_init):
    nb, n_pages = page_table.shape
    npg = _PAGES_PER_STEP
    past_len = n_pages * PAGE_SIZE

    def page_spec(i):
        return pl.BlockSpec((None, None, DA_HEADS, DA_DV, PAGE_SIZE), lambda b, j, pt: (layer, pt[b, j * npg + i], 0, 0, 0))

    col_spec = pl.BlockSpec((1, DA_HEADS, DA_DV, 1), lambda b, j, pt: (b, 0, 0, 0))
    grid_spec = pltpu.PrefetchScalarGridSpec(
        num_scalar_prefetch=1,
        grid=(nb, n_pages // npg),
        in_specs=[
            col_spec, col_spec, col_spec,
            pl.BlockSpec((8, LANES), lambda b, j, pt: (0, 0)),
            pl.BlockSpec((1, LANES), lambda b, j, pt: (0, 0)),
            pl.BlockSpec((DA_DV, 1), lambda b, j, pt: (0, 0)),
        ] + [page_spec(i) for i in range(npg)] + [page_spec(i) for i in range(npg)],
        out_specs=col_spec,
        scratch_shapes=[
            pltpu.VMEM((npg, 16, PAGE_SIZE), F32),
            pltpu.VMEM((16, PAGE_SIZE), F32),
            pltpu.VMEM((16, PAGE_SIZE), F32),
            pltpu.VMEM((2 * DA_HEADS, DA_DV, PAGE_SIZE), F32),
        ],
    )
    return pl.pallas_call(
        functools.partial(_attn_dec_body, past_len=past_len, lam_init=lam_init),
        grid_spec=grid_spec,
        out_shape=jax.ShapeDtypeStruct((nb, DA_HEADS, DA_DV, 1), F32),
        compiler_params=_cparams(("parallel", "arbitrary")),
        name="attn_decode",
    )(page_table, q4, kn4, vn4, slopes8, lam, dnw_col, *([cache_kt] * npg), *([cache_vt] * npg))


def _post_body(ys_ref, y5_ref, ya_ref, x_ref, wo_ref, wo_lo_ref, nw_ref, rhi_ref, rlo_ref, rb_ref, o_ref, *, hp):
    tm = x_ref.shape[0]
    mixed_in = jnp.concatenate([ys_ref[...], y5_ref[...], ya_ref[...]], axis=1)
    if hp:
        xm = x_ref[...] + _dot3(mixed_in, wo_ref[...], wo_lo_ref[...])
    else:
        xm = x_ref[...] + _dot(mixed_in.astype(BF16), wo_ref[...])
    t = _rms(xm, nw_ref[...])
    logits = _dot3(t, rhi_ref[...], rlo_ref[...]) + rb_ref[...]
    lane = lax.broadcasted_iota(jnp.int32, (tm, LANES), 1).astype(F32)
    big = 1e9
    gmask = lane < MOE_GROUPS
    gl = jnp.where(gmask, logits, -jnp.inf)
    gmax = jnp.max(gl, axis=1, keepdims=True)
    gidx = jnp.min(jnp.where(gl == gmax, lane, big), axis=1, keepdims=True)
    gsum = jnp.sum(jnp.where(gmask, jnp.exp(logits - gmax), 0.0), axis=1, keepdims=True)
    gprob = 1.0 / gsum
    e0 = 16.0 + gidx * EXPERTS_PER_GROUP
    emask = (lane >= e0) & (lane < e0 + EXPERTS_PER_GROUP)
    el = jnp.where(emask, logits, -jnp.inf)
    emax = jnp.max(el, axis=1, keepdims=True)
    i1 = jnp.min(jnp.where(el == emax, lane, big), axis=1, keepdims=True)
    el2 = jnp.where(emask & (lane != i1), logits, -jnp.inf)
    emax2 = jnp.max(el2, axis=1, keepdims=True)
    i2 = jnp.min(jnp.where(el2 == emax2, lane, big), axis=1, keepdims=True)
    p2 = jnp.exp(emax2 - emax)
    den = 1.0 + p2
    w1 = (1.0 / den) * gprob
    w2 = (p2 / den) * gprob
    a1 = i1 - e0
    a2 = i2 - e0
    first_low = a1 < a2
    lo = jnp.where(first_low, a1, a2)
    hi = jnp.where(first_low, a2, a1)
    w_lo = jnp.where(first_low, w1, w2)
    w_hi = jnp.where(first_low, w2, w1)
    pair = jnp.where(lo == 0.0, 0.0, jnp.where(lo == 1.0, 3.0, 5.0)) + (hi - lo - 1.0)
    cls = gidx * 6.0 + pair
    meta = jnp.where(lane == 0.0, w_lo, jnp.where(lane == 1.0, w_hi, jnp.where(lane == 2.0, cls, 0.0)))
    o_ref[:, 0:D_MODEL] = xm
    o_ref[:, D_MODEL:ROW_W] = meta


def _post(y_ssd, y_s5, y_da, x2d, wo, wo_lo, nw, rhi, rlo, rb, tm, hp):
    m = x2d.shape[0]
    const2 = lambda i: (0, 0)
    return pl.pallas_call(
        functools.partial(_post_body, hp=hp),
        grid=(m // tm,),
        in_specs=[
            pl.BlockSpec((tm, SSD_INNER), lambda i: (i, 0)),
            pl.BlockSpec((tm, S5_WIDTH), lambda i: (i, 0)),
            pl.BlockSpec((tm, DA_WIDTH), lambda i: (i, 0)),
            pl.BlockSpec((tm, D_MODEL), lambda i: (i, 0)),
            pl.BlockSpec((D_MODEL, D_MODEL), const2),
            pl.BlockSpec((D_MODEL, D_MODEL), const2),
            pl.BlockSpec((1, D_MODEL), const2),
            pl.BlockSpec((D_MODEL, LANES), const2),
            pl.BlockSpec((D_MODEL, LANES), const2),
            pl.BlockSpec((1, LANES), const2),
        ],
        out_specs=pl.BlockSpec((tm, ROW_W), lambda i: (i, 0)),
        out_shape=jax.ShapeDtypeStruct((m, ROW_W), F32),
        compiler_params=_cparams(("parallel",)),
        name="post",
    )(y_ssd, y_s5, y_da, x2d, wo, wo_lo, nw, rhi, rlo, rb)


def _row_copy(src, dst, sem):
    return pltpu.make_async_copy(src, dst, sem)


def _scatter_body(dest_ref, x_ref, init_ref, o_hbm, sem, *, ts):
    del init_ref
    base = pl.program_id(0) * ts

    def issue(r, c):
        d = dest_ref[base + r]
        _row_copy(x_ref.at[pl.ds(r, 1)], o_hbm.at[pl.ds(d, 1)], sem).start()
        return c

    lax.fori_loop(0, ts, issue, 0, unroll=8)
    _row_copy(x_ref, o_hbm.at[pl.ds(0, ts)], sem).wait()


def _moe_scatter(dest, rows, mp, ts):
    m = rows.shape[0]
    init = jnp.zeros((mp, ROW_W), F32)
    grid_spec = pltpu.PrefetchScalarGridSpec(
        num_scalar_prefetch=1,
        grid=(m // ts,),
        in_specs=[
            pl.BlockSpec((ts, ROW_W), lambda i, d: (i, 0)),
            pl.BlockSpec(memory_space=pl.ANY),
        ],
        out_specs=pl.BlockSpec(memory_space=pl.ANY),
        scratch_shapes=[pltpu.SemaphoreType.DMA(())],
    )
    return pl.pallas_call(
        functools.partial(_scatter_body, ts=ts),
        grid_spec=grid_spec,
        out_shape=jax.ShapeDtypeStruct((mp, ROW_W), F32),
        input_output_aliases={2: 0},
        compiler_params=_cparams(("arbitrary",)),
        name="moe_scatter",
    )(dest, rows, init)


def _moe_body(blk_ref, val_ref, e1_ref, e2_ref, x_ref, nw_ref, wg1_ref, wu1_ref, wd1_ref, wg2_ref, wu2_ref, wd2_ref, o_ref):
    del blk_ref, e1_ref, e2_ref
    t = pl.program_id(0)

    @pl.when(val_ref[t] > 0)
    def _():
        x = x_ref[:, 0:D_MODEL]
        w_lo = x_ref[:, D_MODEL:D_MODEL + 1]
        w_hi = x_ref[:, D_MODEL + 1:D_MODEL + 2]
        tb = _rms(x, nw_ref[...]).astype(BF16)

        def expert(wg, wu, wd):
            hid = jax.nn.silu(_dot(tb, wg[0])) * _dot(tb, wu[0])
            return _dot(hid.astype(BF16), wd[0])

        moe = w_lo * expert(wg1_ref, wu1_ref, wd1_ref) + w_hi * expert(wg2_ref, wu2_ref, wd2_ref)
        o_ref[...] = x + moe

    @pl.when(val_ref[t] == 0)
    def _():
        o_ref[...] = jnp.zeros(o_ref.shape, F32)


def _moe_experts(tile_blk, tile_valid, tile_e1, tile_e2, xs, nw, wg, wu, wd, tmo):
    mp = xs.shape[0]
    nt = mp // tmo
    up = lambda sel: pl.BlockSpec((1, D_MODEL, EXPERT_HIDDEN), (lambda t, tb, tv, e1, e2: ((e1, e2)[sel][t], 0, 0)))
    down = lambda sel: pl.BlockSpec((1, EXPERT_HIDDEN, D_MODEL), (lambda t, tb, tv, e1, e2: ((e1, e2)[sel][t], 0, 0)))
    grid_spec = pltpu.PrefetchScalarGridSpec(
        num_scalar_prefetch=4,
        grid=(nt,),
        in_specs=[
            pl.BlockSpec((tmo, ROW_W), lambda t, tb, tv, e1, e2: (tb[t], 0)),
            pl.BlockSpec((1, D_MODEL), lambda t, tb, tv, e1, e2: (0, 0)),
            up(0), up(0), down(0), up(1), up(1), down(1),
        ],
        out_specs=pl.BlockSpec((tmo, D_MODEL), lambda t, tb, tv, e1, e2: (t, 0)),
    )
    return pl.pallas_call(
        _moe_body,
        grid_spec=grid_spec,
        out_shape=jax.ShapeDtypeStruct((mp, D_MODEL), F32),
        compiler_params=_cparams(("arbitrary",)),
        name="moe_experts",
    )(tile_blk, tile_valid, tile_e1, tile_e2, xs, nw, wg, wu, wd, wg, wu, wd)


def _gather_body(dest_ref, ys_hbm, fnw_ref, o_ref, sem, *, ts, final_norm):
    base = pl.program_id(0) * ts

    def issue(r, c):
        d = dest_ref[base + r]
        _row_copy(ys_hbm.at[pl.ds(d, 1)], o_ref.at[pl.ds(r, 1)], sem).start()
        return c

    lax.fori_loop(0, ts, issue, 0, unroll=8)
    _row_copy(ys_hbm.at[pl.ds(0, ts)], o_ref, sem).wait()
    if final_norm:
        o_ref[...] = _rms(o_ref[...], fnw_ref[...])


def _moe_gather(dest, ys, fnw, m, ts, final_norm):
    grid_spec = pltpu.PrefetchScalarGridSpec(
        num_scalar_prefetch=1,
        grid=(m // ts,),
        in_specs=[
            pl.BlockSpec(memory_space=pl.ANY),
            pl.BlockSpec((1, D_MODEL), lambda i, d: (0, 0)),
        ],
        out_specs=pl.BlockSpec((ts, D_MODEL), lambda i, d: (i, 0)),
        scratch_shapes=[pltpu.SemaphoreType.DMA(())],
    )
    return pl.pallas_call(
        functools.partial(_gather_body, ts=ts, final_norm=final_norm),
        grid_spec=grid_spec,
        out_shape=jax.ShapeDtypeStruct((m, D_MODEL), F32),
        compiler_params=_cparams(("arbitrary",)),
        name="moe_gather",
    )(dest, ys, fnw)


def _moe_plan(cls, tmo):
    m = cls.shape[0]
    nt = m // tmo + N_CLASSES
    onehot = (cls[:, None] == jnp.arange(N_CLASSES, dtype=jnp.int32)[None, :]).astype(jnp.int32)
    csum = jnp.cumsum(onehot, axis=0)
    rank = jnp.sum(csum * onehot, axis=1) - 1
    counts = csum[-1]
    ntiles = (counts + tmo - 1) // tmo
    tile_end = jnp.cumsum(ntiles)
    tile_start = tile_end - ntiles
    dest = jnp.sum(onehot * (tile_start * tmo)[None, :], axis=1) + rank
    total = tile_end[-1]
    tid = jnp.arange(nt, dtype=jnp.int32)
    tile_valid = (tid < total).astype(jnp.int32)
    tile_blk = jnp.minimum(tid, total - 1)
    tile_cls = jnp.minimum(jnp.sum((tile_blk[:, None] >= tile_end[None, :]).astype(jnp.int32), axis=1), N_CLASSES - 1)
    grp = tile_cls // 6
    pair = tile_cls % 6
    lo = jnp.asarray(_PAIR_LO, jnp.int32)[pair]
    hi = jnp.asarray(_PAIR_HI, jnp.int32)[pair]
    return dest.astype(jnp.int32), tile_blk, tile_valid, grp * EXPERTS_PER_GROUP + lo, grp * EXPERTS_PER_GROUP + hi


def _moe(rows, nw2, wg, wu, wd, fnw, ts, tmo, final_norm):
    m = rows.shape[0]
    cls = rows[:, D_MODEL + 2].astype(jnp.int32)
    dest, tile_blk, tile_valid, e1, e2 = _moe_plan(cls, tmo)
    mp = m + N_CLASSES * tmo
    xs = _moe_scatter(dest, rows, mp, ts)
    ys = _moe_experts(tile_blk, tile_valid, e1, e2, xs, nw2, wg, wu, wd, tmo)
    return _moe_gather(dest, ys, fnw, m, ts, final_norm)


def _pad_lanes(v, width):
    return jnp.pad(v, ((0, 0), (0, width - v.shape[1])))


def _layer_params(l, p):
    w_in = p["w_in"][l]
    wp = jnp.concatenate([w_in[:, :OFF_DT], _pad_lanes(w_in[:, OFF_DT:OFF_U], LANES), w_in[:, OFF_U:]], axis=1)
    lp = {"norm1": p["norm1_w"][l][None, :]}
    lp["wp"], lp["wp_lo"] = _split2_param(wp)
    lp["cw8"] = jnp.pad(p["conv_w"][l], ((0, 4), (0, 0)))
    lp["cb"] = p["conv_b"][l][None, :]
    lp["dtb"] = _pad_lanes(p["dt_bias"][l][None, :], LANES)
    lp["alog"] = _pad_lanes(p["a_log"][l][None, :], LANES)
    lp["dvec"] = jnp.repeat(p["ssd_d"][l], SSD_HEAD_DIM)[None, :]
    lp["ssd_nw"] = p["ssd_norm_w"][l][None, :]
    lam = lax.complex(p["s5_lambda_re"][l], p["s5_lambda_im"][l])
    step = jnp.exp(p["s5_log_step"][l])[:, None]
    lam_bar = jnp.exp(lam * step)
    b = lax.complex(p["s5_b_re"][l], p["s5_b_im"][l])
    b_bar = ((lam_bar - 1.0) / lam)[..., None] * b
    eye = jnp.eye(S5_GROUPS, dtype=F32)

    def bdiag_in(m_gnc):
        return jnp.einsum("gnc,gh->gchn", m_gnc, eye).reshape(S5_WIDTH, S5_N)

    def bdiag_out(m_gcn):
        return jnp.einsum("gcn,gh->gnhc", m_gcn, eye).reshape(S5_N, S5_WIDTH)

    bmat = jnp.concatenate([bdiag_in(b_bar.real), bdiag_in(b_bar.imag)], axis=1)
    cmat = jnp.concatenate([bdiag_out(p["s5_c_re"][l]), -bdiag_out(p["s5_c_im"][l])], axis=0)
    lp["bhi"], lp["blo"] = _split2_param(bmat)
    lp["chi"], lp["clo"] = _split2_param(cmat)
    a_re = lam_bar.real.reshape(1, S5_N)
    a_im = lam_bar.imag.reshape(1, S5_N)
    lp["are"] = jnp.concatenate([a_re, a_re], axis=1)
    lp["aim"] = jnp.concatenate([-a_im, a_im], axis=1)
    lp["s5_d"] = p["s5_d"][l][None, :]
    lp["wglu"], lp["wglu_lo"] = _split2_param(p["s5_w_glu"][l])
    lp["bglu"] = p["s5_b_glu"][l][None, :]
    lam_init = 0.8 - 0.6 * math.exp(-0.3 * l)
    lam_s = (jnp.exp(jnp.sum(p["lam_q1"][l] * p["lam_k1"][l])) - jnp.exp(jnp.sum(p["lam_q2"][l] * p["lam_k2"][l])) + lam_init)
    lp["lam"] = jnp.full((1, LANES), lam_s, F32)
    lp["lam_init"] = lam_init
    lp["dnw_col"] = p["da_norm_w"][l][:, None]
    lp["wo"], lp["wo_lo"] = _split2_param(p["w_out"][l])
    lp["norm2"] = p["norm2_w"][l][None, :]
    wr = jnp.zeros((D_MODEL, LANES), F32)
    wr = wr.at[:, 0:MOE_GROUPS].set(p["w_router_group"][l]).at[:, 16:16 + N_EXPERTS].set(p["w_router_expert"][l])
    lp["rhi"], lp["rlo"] = _split2_param(wr)
    rb = jnp.zeros((1, LANES), F32)
    lp["rb"] = rb.at[0, 0:MOE_GROUPS].set(p["b_router_group"][l]).at[0, 16:16 + N_EXPERTS].set(p["b_router_expert"][l])
    lp["wg"] = p["w_gate"][l].astype(BF16)
    lp["wu"] = p["w_up"][l].astype(BF16)
    lp["wd"] = p["w_down"][l].astype(BF16)
    return lp


def _mixers(x2d, nb, t_len, lp, conv_buf, ssd_h0, s5_s0, attend, tm_proj, hp):
    z, xbc, dt, u, k, v, *qkv = _inproj(x2d, lp["norm1"], lp["wp"], lp["wp_lo"], tm_proj, hp)
    L = SSD_CHUNK
    tp = -(-t_len // L) * L

    def seq(a):
        a = a.reshape(nb, t_len, a.shape[-1])
        return a if tp == t_len else jnp.pad(a, ((0, 0), (0, tp - t_len), (0, 0)))

    cbuf0 = jnp.pad(conv_buf, ((0, 0), (5, 0), (0, 0)))
    tri = jnp.tril(jnp.ones((L, L), F32)).astype(BF16)
    y_ssd, cnew, ssd_new = _ssd(seq(xbc), seq(z), seq(dt), cbuf0, ssd_h0, lp["cw8"], lp["cb"], lp["dtb"], lp["alog"],
                             lp["dvec"], lp["ssd_nw"], tri, t_len, hp)
    y_ssd = y_ssd[:, :t_len].reshape(nb * t_len, SSD_INNER)
    conv_new = cnew[:, 0:3]

    u_tb = jnp.swapaxes(u.reshape(nb, t_len, S5_WIDTH), 0, 1).reshape(t_len * nb, S5_WIDTH)
    s0 = jnp.concatenate([s5_s0[..., 0].reshape(nb, S5_N), s5_s0[..., 1].reshape(nb, S5_N)], axis=1)
    tc = min(t_len, 64)
    y5_tb, sfin = _s5(u_tb, s0, lp["are"], lp["aim"], lp["bhi"], lp["blo"], lp["chi"], lp["clo"], lp["s5_d"],
                      lp["wglu"], lp["wglu_lo"], lp["bglu"], nb, tc, hp)
    y_s5 = jnp.swapaxes(y5_tb.reshape(t_len, nb, S5_WIDTH), 0, 1).reshape(nb * t_len, S5_WIDTH)
    s5_new = jnp.stack([sfin[:, :S5_N].reshape(nb, S5_GROUPS, S5_STATE), sfin[:, S5_N:].reshape(nb, S5_GROUPS, S5_STATE)],
                       axis=-1)

    y_da = attend(k, v, *qkv)
    k_rows = k.reshape(nb, t_len, DA_HEADS, 2 * DA_DQK)
    v_rows = v.reshape(nb, t_len, DA_HEADS, DA_DV)
    return y_ssd, y_s5, y_da, (k_rows, v_rows, conv_new, ssd_new, s5_new)


def _trunk_layer(x2d, nb, t_len, lp, conv_buf, ssd_h0, s5_s0, attend, fnw, final_norm, tm_proj, ts, tmo, hp):
    y_ssd, y_s5, y_da, states = _mixers(x2d, nb, t_len, lp, conv_buf, ssd_h0, s5_s0, attend, tm_proj, hp)
    rows = _post(y_ssd, y_s5, y_da, x2d, lp["wo"], lp["wo_lo"], lp["norm2"], lp["rhi"], lp["rlo"], lp["rb"], tm_proj, hp)
    x_new = _moe(rows, lp["norm2"], lp["wg"], lp["wu"], lp["wd"], fnw, ts, tmo, final_norm)
    return x_new, states


def kernel(x_prompt, x_sample, cache_k, cache_v, page_table, state_conv, state_ssd, state_s5, norm1_w, w_in, conv_w, conv_b, dt_bias, a_log, ssd_d, ssd_norm_w, s5_lambda_re, s5_lambda_im, s5_log_step, s5_b_re, s5_b_im, s5_c_re, s5_c_im, s5_d, s5_w_glu, s5_b_glu, lam_q1, lam_k1, lam_q2, lam_k2, da_norm_w, w_out, norm2_w, w_router_group, b_router_group, w_router_expert, b_router_expert, w_gate, w_up, w_down, final_norm_w):
    params = dict(
        norm1_w=norm1_w, w_in=w_in, conv_w=conv_w, conv_b=conv_b, dt_bias=dt_bias, a_log=a_log, ssd_d=ssd_d,
        ssd_norm_w=ssd_norm_w, s5_lambda_re=s5_lambda_re, s5_lambda_im=s5_lambda_im, s5_log_step=s5_log_step,
        s5_b_re=s5_b_re, s5_b_im=s5_b_im, s5_c_re=s5_c_re, s5_c_im=s5_c_im, s5_d=s5_d, s5_w_glu=s5_w_glu,
        s5_b_glu=s5_b_glu, lam_q1=lam_q1, lam_k1=lam_k1, lam_q2=lam_q2, lam_k2=lam_k2, da_norm_w=da_norm_w,
        w_out=w_out, norm2_w=norm2_w, w_router_group=w_router_group, b_router_group=b_router_group,
        w_router_expert=w_router_expert, b_router_expert=b_router_expert, w_gate=w_gate, w_up=w_up, w_down=w_down)
    bp, t_p, _ = x_prompt.shape
    bs, t_s, _ = x_sample.shape
    slopes = jnp.exp2(-8.0 * jnp.arange(1, DA_HEADS + 1, dtype=F32) / DA_HEADS)
    slopes8 = jnp.broadcast_to(jnp.pad(slopes, (0, 2))[:, None], (8, LANES))
    cache_kt = jnp.transpose(cache_k, (0, 1, 3, 4, 2))
    cache_vt = jnp.transpose(cache_v, (0, 1, 3, 4, 2))
    fnw = final_norm_w[None, :]

    xp = x_prompt.reshape(bp * t_p, D_MODEL)
    xs = x_sample.reshape(bs * t_s, D_MODEL)
    p_states, s_states = [], []
    for l in range(DEPTH):
        lp = _layer_params(l, params)
        last = l == DEPTH - 1

        hp_prompt = l == 0

        def attend_p(k, v, *rest, lp=lp, hp=hp_prompt):
            r3 = lambda a: a.reshape(bp, t_p, DA_WIDTH)
            qf = r3(rest[0]) if hp else None
            qb, kb, vb = rest[-3:]
            return _attn_prompt(r3(qb), r3(kb), r3(vb), qf, r3(k), r3(v), slopes8, lp["lam"], lp["dnw_col"],
                                lp["lam_init"], 512, hp).reshape(bp * t_p, DA_WIDTH)

        def attend_s(k, v, q, *_, lp=lp, l=l):
            shp = (bs, DA_HEADS, DA_DV, 1)
            return _attn_decode(q.reshape(shp), k.reshape(shp), v.reshape(shp), cache_kt, cache_vt, page_table, slopes8,
                                lp["lam"], lp["dnw_col"], l, lp["lam_init"]).reshape(bs, DA_WIDTH)

        xp, st_p = _trunk_layer(
            xp, bp, t_p, lp,
            jnp.zeros((bp, SSD_CONV - 1, SSD_XBC), F32),
            jnp.zeros((bp, SSD_HEADS, SSD_HEAD_DIM, SSD_STATE), F32),
            jnp.zeros((bp, S5_GROUPS, S5_STATE, 2), F32),
            attend_p, fnw, last, 512, 256, 256, hp_prompt)
        p_states.append(st_p)
        xs, st_s = _trunk_layer(xs, bs, t_s, lp, state_conv[l], state_ssd[l], state_s5[l], attend_s, fnw, last, 32, 32, 16, True)
        s_states.append(st_s)
    k_p, v_p, conv_p, ssd_p, s5_p = [jnp.stack([st[i] for st in p_states]) for i in range(5)]
    k_s, v_s, conv_s, ssd_s, s5_s = [jnp.stack([st[i] for st in s_states]) for i in range(5)]
    y_prompt = xp.reshape(bp, t_p, D_MODEL)
    y_sample = xs.reshape(bs, t_s, D_MODEL)
    return (y_prompt, y_sample, k_p, v_p, k_s, v_s, conv_p, conv_s, ssd_p, ssd_s, s5_p, s5_s)
```

```python
import functools
import math

import jax
import jax.numpy as jnp
import numpy as np
from jax import lax
from jax.experimental import pallas as pl
from jax.experimental.pallas import tpu as pltpu

F32 = jnp.float32
BF16 = jnp.bfloat16

D_MODEL = 1024
DEPTH = 2
PAGE_SIZE = 128
SSD_HEAD_DIM = 64
SSD_INNER = 384
SSD_HEADS = 6
SSD_GROUPS = 2
SSD_STATE = 64
SSD_CONV = 4
SSD_XBC = 640
SSD_CHUNK = 128
DA_DQK = 32
DA_DV = 64
DA_WIDTH = 384
DA_HEADS = 6
S5_WIDTH = 256
S5_GROUP_CH = 16
S5_GROUPS = 16
S5_STATE = 64
S5_N = S5_GROUPS * S5_STATE
OFF_XBC = 384
OFF_DT = 1024
OFF_U = 1030
OFF_Q = 1286
OFF_K = 1670
OFF_V = 2054
N_IN = 2438
N_IN_PAD = 2560
MOE_GROUPS = 4
EXPERTS_PER_GROUP = 4
N_EXPERTS = 16
N_CLASSES = 24
EXPERT_HIDDEN = 512
RMS_EPS = 1e-6
LANES = 128
META_W = 128
ROW_W = D_MODEL + META_W
VMEM_LIMIT = 56 * 1024 * 1024
NEG_BIG = -1e30
LOG2E = 1.4426950408889634
C_QK = (DA_DQK ** -0.5) * LOG2E

_PAIR_LO = (0, 0, 0, 1, 1, 2)
_PAIR_HI = (1, 2, 3, 2, 3, 3)


def _cparams(sem):
    return pltpu.CompilerParams(dimension_semantics=sem, vmem_limit_bytes=VMEM_LIMIT)


def _rms(x, w):
    ms = jnp.mean(x * x, axis=-1, keepdims=True)
    return (x * lax.rsqrt(ms + RMS_EPS)) * w


def _softplus(x):
    return jnp.maximum(x, 0.0) + jnp.log(1.0 + jnp.exp(-jnp.abs(x)))


def _split2(a):
    hi = a.astype(BF16)
    lo = (a - hi.astype(F32)).astype(BF16)
    return hi, lo


def _split2_param(a):
    hi = lax.reduce_precision(a, exponent_bits=8, mantissa_bits=7)
    return hi.astype(BF16), (a - hi).astype(BF16)


def _dot(a, b):
    return jnp.dot(a, b, preferred_element_type=F32)


def _dot_nt(a, b):
    return lax.dot_general(a, b, (((1,), (1,)), ((), ())), preferred_element_type=F32)


def _dot_tn(a, b):
    return lax.dot_general(a, b, (((0,), (0,)), ((), ())), preferred_element_type=F32)


def _mm(a, b, hp, nt=False):
    d = _dot_nt if nt else _dot
    if not hp:
        return d(a, b)
    a_hi, a_lo = _split2(a)
    b_hi, b_lo = _split2(b)
    return d(a_hi, b_hi) + (d(a_lo, b_hi) + d(a_hi, b_lo))


def _dot3(a, b_hi, b_lo):
    a_hi, a_lo = _split2(a)
    return _dot(a_hi, b_hi) + (_dot(a_lo, b_hi) + _dot(a_hi, b_lo))


def _inproj_body(x_ref, nw_ref, *refs, hp, qscale):
    h = _rms(x_ref[...], nw_ref[...])
    if hp:
        whi_ref, wlo_ref = refs[:2]
        outs = refs[2:]
        p = _dot3(h, whi_ref[...], wlo_ref[...])
    else:
        outs = refs[1:]
        p = _dot(h.astype(BF16), refs[0][...])
    z_ref, xbc_ref, dt_ref, u_ref, k_ref, v_ref = outs[:6]
    z_ref[...] = p[:, 0:384]
    xbc_ref[...] = p[:, 384:1024]
    dt_ref[...] = p[:, 1024:1152]
    u_ref[...] = p[:, 1152:1408]
    q = p[:, 1408:1792] * qscale
    k = p[:, 1792:2176]
    v = p[:, 2176:2560]
    k_ref[...] = k
    v_ref[...] = v
    bf = outs[6:]
    if hp:
        outs[6][...] = q
        bf = outs[7:]
    bf[0][...] = q.astype(BF16)
    bf[1][...] = k.astype(BF16)
    bf[2][...] = v.astype(BF16)


def _inproj(x2d, nw, w_hi, w_lo, tm, hp, qscale):
    m = x2d.shape[0]
    widths = (384, 640, 128, 256, 384, 384) + ((384,) if hp else ()) + (384, 384, 384)
    dtypes = (F32,) * 6 + ((F32,) if hp else ()) + (BF16, BF16, BF16)
    w_spec = pl.BlockSpec((D_MODEL, N_IN_PAD), lambda i: (0, 0))
    weights = (w_hi, w_lo) if hp else (w_hi,)
    return pl.pallas_call(
        functools.partial(_inproj_body, hp=hp, qscale=qscale),
        grid=(m // tm,),
        in_specs=[
            pl.BlockSpec((tm, D_MODEL), lambda i: (i, 0)),
            pl.BlockSpec((1, D_MODEL), lambda i: (0, 0)),
        ] + [w_spec] * len(weights),
        out_specs=[pl.BlockSpec((tm, w), lambda i: (i, 0)) for w in widths],
        out_shape=[jax.ShapeDtypeStruct((m, w), dt) for w, dt in zip(widths, dtypes)],
        compiler_params=_cparams(("parallel",)),
        name="inproj",
    )(x2d, nw, *weights)


def _ssd_body(xbc_ref, z_ref, dt_ref, cbuf0_ref, h0_ref, cw_ref, cb_ref, dtb_ref, alog_ref, dvec_ref, nw_ref, tri_ref,
              y_ref, cnew_ref, hfin_ref, cbuf, hs, *, t_valid, hp):
    L = SSD_CHUNK
    c = pl.program_id(1)
    nc = pl.num_programs(1)

    @pl.when(c == 0)
    def _():
        cbuf[0:8, :] = cbuf0_ref[0]
        hs[...] = jnp.zeros(hs.shape, F32)
        for h in range(SSD_HEADS):
            g = h // 3
            hs[h * 64:(h + 1) * 64, g * 64:(g + 1) * 64] = h0_ref[0, h]
        cnew_ref[0] = jnp.zeros((8, SSD_XBC), F32)

    cbuf[8:8 + L, :] = xbc_ref[0]
    acc = cb_ref[...] + cbuf[5:5 + L, :] * cw_ref[0:1, :]
    acc = acc + cbuf[6:6 + L, :] * cw_ref[1:2, :]
    acc = acc + cbuf[7:7 + L, :] * cw_ref[2:3, :]
    acc = acc + cbuf[8:8 + L, :] * cw_ref[3:4, :]
    act = jax.nn.silu(acc)

    c_last = (t_valid - 1) // L
    r_last = (t_valid - 1) % L

    @pl.when(c == c_last)
    def _():
        cnew_ref[0, 0:3, :] = cbuf[6 + r_last:9 + r_last, :]

    tail = cbuf[L:L + 8, :]
    cbuf[0:8, :] = tail

    xs = act[:, 0:384]
    bm = act[:, 384:512]
    cm = act[:, 512:640]

    rows = lax.broadcasted_iota(jnp.int32, (L, LANES), 0) + c * L
    dtv = _softplus(dt_ref[0] + dtb_ref[...])
    dtv = jnp.where(rows < t_valid, dtv, 0.0)
    a = -jnp.exp(alog_ref[...])
    dta = dtv * a
    tri = tri_ref[...]
    d_hi = dta.astype(BF16)
    r1 = dta - d_hi.astype(F32)
    d_mid = r1.astype(BF16)
    d_lo = (r1 - d_mid.astype(F32)).astype(BF16)
    acum = _dot(tri, d_hi) + (_dot(tri, d_mid) + _dot(tri, d_lo))
    acum_t = acum.T
    dt_t = dtv.T
    xs_t = xs.T

    lane = lax.broadcasted_iota(jnp.int32, (L, LANES), 1)
    low = lane < 64
    li = lax.broadcasted_iota(jnp.int32, (L, L), 0)
    si = lax.broadcasted_iota(jnp.int32, (L, L), 1)
    causal = li >= si
    pre = (lambda a: a) if hp else (lambda a: a.astype(BF16))
    bm_b = pre(bm)
    cm_b = pre(cm)
    xs_b = pre(xs)
    xs_tb = pre(xs_t)
    cb_g = [
        _mm(pre(jnp.where(low, cm, 0.0)), bm_b, hp, nt=True),
        _mm(pre(jnp.where(low, 0.0, cm)), bm_b, hp, nt=True),
    ]
    lane_h = lax.broadcasted_iota(jnp.int32, (64, LANES), 1)

    y_pairs = []
    for p in range(3):
        halves = []
        ecols = []
        for half in range(2):
            h = 2 * p + half
            g = h // 3
            col = acum[:, h:h + 1]
            row = acum_t[h:h + 1, :]
            decay = jnp.where(causal, jnp.exp(col - row), 0.0)
            mh = (cb_g[g] * decay) * dt_t[h:h + 1, :]
            halves.append(_mm(pre(mh), xs_b[:, p * 128:(p + 1) * 128], hp))
            ecols.append(jnp.exp(col))
        y_diag = jnp.where(low, halves[0], halves[1])
        h_in = hs[p * 128:(p + 1) * 128, :]
        y_off = _mm(cm_b, pre(h_in), hp, nt=True) * jnp.where(low, ecols[0], ecols[1])
        y_pairs.append(y_diag + y_off)
        for half in range(2):
            h = 2 * p + half
            g = h // 3
            col = acum[:, h:h + 1]
            last = acum[L - 1:L, h:h + 1]
            w = jnp.exp(last - col) * dtv[:, h:h + 1]
            st = _mm(xs_tb[h * 64:(h + 1) * 64, :], pre(bm * w), hp)
            gmask = (lane_h < 64) if g == 0 else (lane_h >= 64)
            st = jnp.where(gmask, st, 0.0)
            hs[h * 64:(h + 1) * 64, :] = jnp.exp(last) * hs[h * 64:(h + 1) * 64, :] + st

    y = jnp.concatenate(y_pairs, axis=1) + dvec_ref[...] * xs
    y = y * jax.nn.silu(z_ref[0])
    y_ref[0] = _rms(y, nw_ref[...])

    @pl.when(c == nc - 1)
    def _():
        for h in range(SSD_HEADS):
            g = h // 3
            hfin_ref[0, h] = hs[h * 64:(h + 1) * 64, g * 64:(g + 1) * 64]


def _ssd(xbc, z, dt, cbuf0, h0, cw8, cb, dtb, alog, dvec, nw, tri, t_valid, hp):
    b, tp, _ = xbc.shape
    L = SSD_CHUNK
    const2 = lambda i, j: (0, 0)
    return pl.pallas_call(
        functools.partial(_ssd_body, t_valid=t_valid, hp=hp),
        grid=(b, tp // L),
        in_specs=[
            pl.BlockSpec((1, L, SSD_XBC), lambda i, j: (i, j, 0)),
            pl.BlockSpec((1, L, SSD_INNER), lambda i, j: (i, j, 0)),
            pl.BlockSpec((1, L, LANES), lambda i, j: (i, j, 0)),
            pl.BlockSpec((1, 8, SSD_XBC), lambda i, j: (i, 0, 0)),
            pl.BlockSpec((1, SSD_HEADS, SSD_HEAD_DIM, SSD_STATE), lambda i, j: (i, 0, 0, 0)),
            pl.BlockSpec((8, SSD_XBC), const2),
            pl.BlockSpec((1, SSD_XBC), const2),
            pl.BlockSpec((1, LANES), const2),
            pl.BlockSpec((1, LANES), const2),
            pl.BlockSpec((1, SSD_INNER), const2),
            pl.BlockSpec((1, SSD_INNER), const2),
            pl.BlockSpec((L, L), const2),
        ],
        out_specs=[
            pl.BlockSpec((1, L, SSD_INNER), lambda i, j: (i, j, 0)),
            pl.BlockSpec((1, 8, SSD_XBC), lambda i, j: (i, 0, 0)),
            pl.BlockSpec((1, SSD_HEADS, SSD_HEAD_DIM, SSD_STATE), lambda i, j: (i, 0, 0, 0)),
        ],
        out_shape=[
            jax.ShapeDtypeStruct((b, tp, SSD_INNER), F32),
            jax.ShapeDtypeStruct((b, 8, SSD_XBC), F32),
            jax.ShapeDtypeStruct((b, SSD_HEADS, SSD_HEAD_DIM, SSD_STATE), F32),
        ],
        scratch_shapes=[pltpu.VMEM((L + 8, SSD_XBC), F32), pltpu.VMEM((SSD_INNER, LANES), F32)],
        compiler_params=_cparams(("parallel", "arbitrary")),
        name="ssd",
    )(xbc, z, dt, cbuf0, h0, cw8, cb, dtb, alog, dvec, nw, tri)


def _s5_body(u_ref, s0_ref, are_ref, aim_ref, bhi_ref, blo_ref, chi_ref, clo_ref, dvec_ref, wglu_ref, wglu_lo_ref, bglu_ref,
             y_ref, sfin_ref, st, carry, *, nb, tc, hp):
    i = pl.program_id(0)

    @pl.when(i == 0)
    def _():
        carry[...] = s0_ref[...]

    u = u_ref[...]
    if hp:
        st[...] = _dot3(u, bhi_ref[...], blo_ref[...])
    else:
        st[...] = _dot(u.astype(BF16), bhi_ref[...])
    are = jnp.broadcast_to(are_ref[...], (nb, 2 * S5_N))
    aim = jnp.broadcast_to(aim_ref[...], (nb, 2 * S5_N))

    def step(t, s):
        r = pl.multiple_of(t * nb, nb)
        sw = jnp.concatenate([s[:, S5_N:], s[:, :S5_N]], axis=1)
        s = are * s + aim * sw + st[pl.ds(r, nb), :]
        st[pl.ds(r, nb), :] = s
        return s

    s_last = lax.fori_loop(0, tc, step, carry[...])
    carry[...] = s_last
    sfin_ref[...] = s_last

    if hp:
        y = _dot3(st[...], chi_ref[...], clo_ref[...]) + dvec_ref[...] * u
    else:
        y = _dot(st[...].astype(BF16), chi_ref[...]) + dvec_ref[...] * u
    y = jax.nn.gelu(y)
    if hp:
        g = _dot3(y, wglu_ref[...], wglu_lo_ref[...]) + bglu_ref[...]
    else:
        g = _dot(y.astype(BF16), wglu_ref[...]) + bglu_ref[...]
    y_ref[...] = g[:, :S5_WIDTH] * jax.nn.sigmoid(g[:, S5_WIDTH:])


def _s5(u_tb, s0, are, aim, bhi, blo, chi, clo, dvec, wglu, wglu_lo, bglu, nb, tc, hp):
    rows = u_tb.shape[0]
    blk = tc * nb
    const2 = lambda i: (0, 0)
    return pl.pallas_call(
        functools.partial(_s5_body, nb=nb, tc=tc, hp=hp),
        grid=(rows // blk,),
        in_specs=[
            pl.BlockSpec((blk, S5_WIDTH), lambda i: (i, 0)),
            pl.BlockSpec((nb, 2 * S5_N), const2),
            pl.BlockSpec((1, 2 * S5_N), const2),
            pl.BlockSpec((1, 2 * S5_N), const2),
            pl.BlockSpec((S5_WIDTH, 2 * S5_N), const2),
            pl.BlockSpec((S5_WIDTH, 2 * S5_N), const2),
            pl.BlockSpec((2 * S5_N, S5_WIDTH), const2),
            pl.BlockSpec((2 * S5_N, S5_WIDTH), const2),
            pl.BlockSpec((1, S5_WIDTH), const2),
            pl.BlockSpec((S5_WIDTH, 2 * S5_WIDTH), const2),
            pl.BlockSpec((S5_WIDTH, 2 * S5_WIDTH), const2),
            pl.BlockSpec((1, 2 * S5_WIDTH), const2),
        ],
        out_specs=[
            pl.BlockSpec((blk, S5_WIDTH), lambda i: (i, 0)),
            pl.BlockSpec((nb, 2 * S5_N), const2),
        ],
        out_shape=[
            jax.ShapeDtypeStruct((rows, S5_WIDTH), F32),
            jax.ShapeDtypeStruct((nb, 2 * S5_N), F32),
        ],
        scratch_shapes=[pltpu.VMEM((blk, 2 * S5_N), F32), pltpu.VMEM((nb, 2 * S5_N), F32)],
        compiler_params=_cparams(("arbitrary",)),
        name="s5",
    )(u_tb, s0, are, aim, bhi, blo, chi, clo, dvec, wglu, wglu_lo, bglu)


def _attn_body(qi_ref, kj_ref, q_ref, k_ref, v_ref, *refs, tq, nq, lam_init, hp):
    if hp:
        qf_ref, kf_ref, vf_ref = refs[:3]
        refs = refs[3:]
    slope_ref, lam_ref, dnw_ref, o_ref, m_sc, l_sc, acc_sc, kx_sc = refs
    p = pl.program_id(1)
    s = pl.program_id(2)
    qi = qi_ref[s]
    kj = kj_ref[s]

    @pl.when(s == 0)
    def _():
        kl = lax.broadcasted_iota(jnp.int32, (tq, LANES), 0).astype(F32)
        lane = lax.broadcasted_iota(jnp.int32, (tq, LANES), 1)
        cols = jnp.zeros((tq, LANES), F32)
        for half in range(2):
            b = (slope_ref[pl.ds(2 * p + half, 1), 0:1] * LOG2E) * kl
            b_hi = b.astype(BF16).astype(F32)
            r1 = b - b_hi
            b_mid = r1.astype(BF16).astype(F32)
            b_lo = r1 - b_mid
            cols = jnp.where(lane == 3 * half, b_hi, jnp.where(lane == 3 * half + 1, b_mid,
                                                                jnp.where(lane == 3 * half + 2, b_lo, cols)))
        kx_sc[...] = cols.astype(BF16)

    @pl.when(kj == 0)
    def _():
        m_sc[...] = jnp.full(m_sc.shape, NEG_BIG, F32)
        l_sc[...] = jnp.zeros(l_sc.shape, F32)
        acc_sc[...] = jnp.zeros(acc_sc.shape, F32)

    def block(diag, hp):
        q = (qf_ref if hp else q_ref)[0]
        k = (kf_ref if hp else k_ref)[0]
        v = (vf_ref if hp else v_ref)[0]
        qlane = lax.broadcasted_iota(jnp.int32, (tq, LANES), 1)
        zero_q = jnp.zeros_like(q)
        kx = kx_sc[...]
        if hp:
            k_hi, k_lo = _split2(k)
            v_hi, v_lo = _split2(v)
            k_cat = jnp.concatenate([k_hi, k_lo, k_hi, kx], axis=1)
        else:
            k_cat = jnp.concatenate([k, kx], axis=1)
        if diag:
            causal = lax.broadcasted_iota(jnp.int32, (tq, tq), 0) <= lax.broadcasted_iota(jnp.int32, (tq, tq), 1)
        off = ((qi - kj) * tq).astype(F32)
        for half in range(2):
            cblk = (slope_ref[pl.ds(2 * p + half, 1), 0:1] * (-LOG2E)) * off
            qx = jnp.where((qlane >= 3 * half) & (qlane < 3 * half + 3), 1.0, 0.0).astype(BF16)
            for mp in range(2):
                cidx = half * 2 + mp
                lo = half * 64 + mp * 32
                qm = jnp.where((qlane >= lo) & (qlane < lo + 32), q, zero_q)
                if hp:
                    qm_hi, qm_lo = _split2(qm)
                    sc = _dot_nt(k_cat, jnp.concatenate([qm_hi, qm_hi, qm_lo, qx], axis=1))
                else:
                    sc = _dot_nt(k_cat, jnp.concatenate([qm, qx], axis=1))
                if diag:
                    sc = jnp.where(causal, sc, -jnp.inf)
                m_prev = m_sc[cidx:cidx + 1, :]
                m_new = jnp.maximum(m_prev, jnp.max(sc, axis=0, keepdims=True) + cblk)
                alpha = jnp.exp2(m_prev - m_new)
                pr = jnp.exp2(sc - (m_new - cblk))
                l_sc[cidx:cidx + 1, :] = alpha * l_sc[cidx:cidx + 1, :] + jnp.sum(pr, axis=0, keepdims=True)
                m_sc[cidx:cidx + 1, :] = m_new
                if hp:
                    p_hi, p_lo = _split2(pr)
                    pv = _dot_tn(v_hi, p_hi) + (_dot_tn(v_lo, p_hi) + _dot_tn(v_hi, p_lo))
                else:
                    pv = _dot_tn(v, pr.astype(BF16))
                r0 = half * 64
                acc_sc[mp, r0:r0 + 64, :] = alpha * acc_sc[mp, r0:r0 + 64, :] + pv[r0:r0 + 64, :]

    precise = (qi == nq - 1) if hp else False
    coarse = (qi != nq - 1) if hp else True

    @pl.when((kj != qi) & coarse)
    def _():
        block(False, False)

    @pl.when((kj == qi) & coarse)
    def _():
        block(True, False)

    if hp:
        @pl.when((kj != qi) & precise)
        def _():
            block(False, True)

        @pl.when((kj == qi) & precise)
        def _():
            block(True, True)

    @pl.when(kj == qi)
    def _():
        lam = lam_ref[0:1, 0:1]
        outs = []
        for half in range(2):
            r0 = half * 64
            o0 = acc_sc[0, r0:r0 + 64, :] / l_sc[half * 2:half * 2 + 1, :]
            o1 = acc_sc[1, r0:r0 + 64, :] / l_sc[half * 2 + 1:half * 2 + 2, :]
            o = o0 - lam * o1
            ms = jnp.mean(o * o, axis=0, keepdims=True)
            outs.append((o * lax.rsqrt(ms + RMS_EPS)) * dnw_ref[...] * (1.0 - lam_init))
        o_ref[0] = jnp.concatenate(outs, axis=0).T


def _attn_prompt(qb, kb, vb, qf, kf, vf, slopes, lam, dnw, lam_init, tq, hp):
    b, t, _ = qb.shape
    nq = t // tq
    qi_tab = np.array([i for i in range(nq) for _ in range(i + 1)], np.int32)
    kj_tab = np.array([j for i in range(nq) for j in range(i + 1)], np.int32)
    const2 = lambda i, p, s, qt, kt: (0, 0)
    in_specs = [
        pl.BlockSpec((1, tq, LANES), lambda i, p, s, qt, kt: (i, qt[s], p)),
        pl.BlockSpec((1, tq, LANES), lambda i, p, s, qt, kt: (i, kt[s], p)),
        pl.BlockSpec((1, tq, LANES), lambda i, p, s, qt, kt: (i, kt[s], p)),
    ]
    operands = [qb, kb, vb]
    if hp:
        kv_last = lambda i, p, s, qt, kt: (i, jnp.where(qt[s] == nq - 1, kt[s], 0), p)
        in_specs += [
            pl.BlockSpec((1, tq, LANES), lambda i, p, s, qt, kt: (i, nq - 1, p)),
            pl.BlockSpec((1, tq, LANES), kv_last),
            pl.BlockSpec((1, tq, LANES), kv_last),
        ]
        operands += [qf, kf, vf]
    in_specs += [pl.BlockSpec((8, LANES), const2), pl.BlockSpec((1, LANES), const2), pl.BlockSpec((64, 1), const2)]
    grid_spec = pltpu.PrefetchScalarGridSpec(
        num_scalar_prefetch=2,
        grid=(b, 3, len(qi_tab)),
        in_specs=in_specs,
        out_specs=pl.BlockSpec((1, tq, LANES), lambda i, p, s, qt, kt: (i, qt[s], p)),
        scratch_shapes=[pltpu.VMEM((4, tq), F32), pltpu.VMEM((4, tq), F32), pltpu.VMEM((2, LANES, tq), F32),
                        pltpu.VMEM((tq, LANES), BF16)],
    )
    return pl.pallas_call(
        functools.partial(_attn_body, tq=tq, nq=nq, lam_init=lam_init, hp=hp),
        grid_spec=grid_spec,
        out_shape=jax.ShapeDtypeStruct((b, t, DA_WIDTH), F32),
        compiler_params=_cparams(("parallel", "parallel", "arbitrary")),
        name="attn_prompt",
    )(jnp.asarray(qi_tab), jnp.asarray(kj_tab), *operands, slopes, lam, dnw)


_PAGES_PER_STEP = 16


def _attn_dec_body(pt_ref, q_ref, kn_ref, vn_ref, slope_ref, lam_ref, dnw_ref, *refs, past_len, lam_init):
    del pt_ref
    npg = _PAGES_PER_STEP
    k_refs = refs[:npg]
    v_refs = refs[npg:2 * npg]
    o_ref, s_sc, m_sc, l_sc, acc_sc = refs[2 * npg:]
    j = pl.program_id(1)
    nj = pl.num_programs(1)
    nrow = 2 * DA_HEADS

    @pl.when(j == 0)
    def _():
        m_sc[...] = jnp.full(m_sc.shape, NEG_BIG, F32)
        l_sc[...] = jnp.zeros(l_sc.shape, F32)
        acc_sc[...] = jnp.zeros(acc_sc.shape, F32)

    scale = DA_DQK ** -0.5
    lane_f = lax.broadcasted_iota(jnp.int32, (1, PAGE_SIZE), 1).astype(F32)
    mx = [None] * nrow
    for i in range(npg):
        dist = (past_len - (j * npg + i) * PAGE_SIZE).astype(F32) - lane_f
        for h in range(DA_HEADS):
            prod = k_refs[i][h] * q_ref[0, h]
            bias = slope_ref[h:h + 1, :] * dist
            for mp in range(2):
                r = 2 * h + mp
                sc = jnp.sum(prod[32 * mp:32 * mp + 32], axis=0, keepdims=True) * scale - bias
                s_sc[i, r:r + 1, :] = sc
                mx[r] = sc if mx[r] is None else jnp.maximum(mx[r], sc)
    for r in range(nrow):
        h = r // 2
        m_prev = m_sc[r:r + 1, :]
        m_new = jnp.maximum(m_prev, jnp.max(mx[r], axis=1, keepdims=True))
        alpha = jnp.exp(m_prev - m_new)
        m_sc[r:r + 1, :] = m_new
        lsum = alpha * l_sc[r:r + 1, :]
        acc = alpha * acc_sc[r]
        for i in range(npg):
            pr = jnp.exp(s_sc[i, r:r + 1, :] - m_new)
            lsum = lsum + pr
            acc = acc + pr * v_refs[i][h]
        l_sc[r:r + 1, :] = lsum
        acc_sc[r] = acc

    @pl.when(j == nj - 1)
    def _():
        lam = lam_ref[0:1, 0:1]
        for h in range(DA_HEADS):
            qk = q_ref[0, h] * kn_ref[0, h]
            ws = []
            for mp in range(2):
                r = 2 * h + mp
                s_new = jnp.sum(qk[32 * mp:32 * mp + 32], axis=0, keepdims=True) * scale
                m_prev = m_sc[r:r + 1, 0:1]
                m_fin = jnp.maximum(m_prev, s_new)
                alpha = jnp.exp(m_prev - m_fin)
                p_new = jnp.exp(s_new - m_fin)
                l_tot = alpha * jnp.sum(l_sc[r:r + 1, :], axis=1, keepdims=True) + p_new
                out = alpha * jnp.sum(acc_sc[r], axis=1, keepdims=True) + p_new * vn_ref[0, h]
                ws.append(out / l_tot)
            o = ws[0] - lam * ws[1]
            ms = jnp.mean(o * o, axis=0, keepdims=True)
            o_ref[0, h] = (o * lax.rsqrt(ms + RMS_EPS)) * dnw_ref[...] * (1.0 - lam_init)


def _attn_decode(q4, kn4, vn4, cache_kt, cache_vt, page_table, slopes8, lam, dnw_col, layer, lam_init):
    nb, n_pages = page_table.shape
    npg = _PAGES_PER_STEP
    past_len = n_pages * PAGE_SIZE

    def page_spec(i):
        return pl.BlockSpec((None, None, DA_HEADS, DA_DV, PAGE_SIZE), lambda b, j, pt: (layer, pt[b, j * npg + i], 0, 0, 0))

    col_spec = pl.BlockSpec((1, DA_HEADS, DA_DV, 1), lambda b, j, pt: (b, 0, 0, 0))
    grid_spec = pltpu.PrefetchScalarGridSpec(
        num_scalar_prefetch=1,
        grid=(nb, n_pages // npg),
        in_specs=[
            col_spec, col_spec, col_spec,
            pl.BlockSpec((8, LANES), lambda b, j, pt: (0, 0)),
            pl.BlockSpec((1, LANES), lambda b, j, pt: (0, 0)),
            pl.BlockSpec((DA_DV, 1), lambda b, j, pt: (0, 0)),
        ] + [page_spec(i) for i in range(npg)] + [page_spec(i) for i in range(npg)],
        out_specs=col_spec,
        scratch_shapes=[
            pltpu.VMEM((npg, 16, PAGE_SIZE), F32),
            pltpu.VMEM((16, PAGE_SIZE), F32),
            pltpu.VMEM((16, PAGE_SIZE), F32),
            pltpu.VMEM((2 * DA_HEADS, DA_DV, PAGE_SIZE), F32),
        ],
    )
    return pl.pallas_call(
        functools.partial(_attn_dec_body, past_len=past_len, lam_init=lam_init),
        grid_spec=grid_spec,
        out_shape=jax.ShapeDtypeStruct((nb, DA_HEADS, DA_DV, 1), F32),
        compiler_params=_cparams(("parallel", "arbitrary")),
        name="attn_decode",
    )(page_table, q4, kn4, vn4, slopes8, lam, dnw_col, *([cache_kt] * npg), *([cache_vt] * npg))


def _post_body(ys_ref, y5_ref, ya_ref, x_ref, wo_ref, wo_lo_ref, nw_ref, rhi_ref, rlo_ref, rb_ref, o_ref, *, hp):
    tm = x_ref.shape[0]
    mixed_in = jnp.concatenate([ys_ref[...], y5_ref[...], ya_ref[...]], axis=1)
    if hp:
        xm = x_ref[...] + _dot3(mixed_in, wo_ref[...], wo_lo_ref[...])
    else:
        xm = x_ref[...] + _dot(mixed_in.astype(BF16), wo_ref[...])
    t = _rms(xm, nw_ref[...])
    logits = _dot3(t, rhi_ref[...], rlo_ref[...]) + rb_ref[...]
    lane = lax.broadcasted_iota(jnp.int32, (tm, LANES), 1).astype(F32)
    big = 1e9
    gmask = lane < MOE_GROUPS
    gl = jnp.where(gmask, logits, -jnp.inf)
    gmax = jnp.max(gl, axis=1, keepdims=True)
    gidx = jnp.min(jnp.where(gl == gmax, lane, big), axis=1, keepdims=True)
    gsum = jnp.sum(jnp.where(gmask, jnp.exp(logits - gmax), 0.0), axis=1, keepdims=True)
    gprob = 1.0 / gsum
    e0 = 16.0 + gidx * EXPERTS_PER_GROUP
    emask = (lane >= e0) & (lane < e0 + EXPERTS_PER_GROUP)
    el = jnp.where(emask, logits, -jnp.inf)
    emax = jnp.max(el, axis=1, keepdims=True)
    i1 = jnp.min(jnp.where(el == emax, lane, big), axis=1, keepdims=True)
    el2 = jnp.where(emask & (lane != i1), logits, -jnp.inf)
    emax2 = jnp.max(el2, axis=1, keepdims=True)
    i2 = jnp.min(jnp.where(el2 == emax2, lane, big), axis=1, keepdims=True)
    p2 = jnp.exp(emax2 - emax)
    den = 1.0 + p2
    w1 = (1.0 / den) * gprob
    w2 = (p2 / den) * gprob
    a1 = i1 - e0
    a2 = i2 - e0
    first_low = a1 < a2
    lo = jnp.where(first_low, a1, a2)
    hi = jnp.where(first_low, a2, a1)
    w_lo = jnp.where(first_low, w1, w2)
    w_hi = jnp.where(first_low, w2, w1)
    pair = jnp.where(lo == 0.0, 0.0, jnp.where(lo == 1.0, 3.0, 5.0)) + (hi - lo - 1.0)
    cls = gidx * 6.0 + pair
    meta = jnp.where(lane == 0.0, w_lo, jnp.where(lane == 1.0, w_hi, jnp.where(lane == 2.0, cls, 0.0)))
    o_ref[:, 0:D_MODEL] = xm
    o_ref[:, D_MODEL:ROW_W] = meta


def _post(y_ssd, y_s5, y_da, x2d, wo, wo_lo, nw, rhi, rlo, rb, tm, hp):
    m = x2d.shape[0]
    const2 = lambda i: (0, 0)
    return pl.pallas_call(
        functools.partial(_post_body, hp=hp),
        grid=(m // tm,),
        in_specs=[
            pl.BlockSpec((tm, SSD_INNER), lambda i: (i, 0)),
            pl.BlockSpec((tm, S5_WIDTH), lambda i: (i, 0)),
            pl.BlockSpec((tm, DA_WIDTH), lambda i: (i, 0)),
            pl.BlockSpec((tm, D_MODEL), lambda i: (i, 0)),
            pl.BlockSpec((D_MODEL, D_MODEL), const2),
            pl.BlockSpec((D_MODEL, D_MODEL), const2),
            pl.BlockSpec((1, D_MODEL), const2),
            pl.BlockSpec((D_MODEL, LANES), const2),
            pl.BlockSpec((D_MODEL, LANES), const2),
            pl.BlockSpec((1, LANES), const2),
        ],
        out_specs=pl.BlockSpec((tm, ROW_W), lambda i: (i, 0)),
        out_shape=jax.ShapeDtypeStruct((m, ROW_W), F32),
        compiler_params=_cparams(("parallel",)),
        name="post",
    )(y_ssd, y_s5, y_da, x2d, wo, wo_lo, nw, rhi, rlo, rb)


def _row_copy(src, dst, sem):
    return pltpu.make_async_copy(src, dst, sem)


def _scatter_body(dest_ref, x_ref, init_ref, o_hbm, sem, *, ts):
    del init_ref
    base = pl.program_id(0) * ts

    def issue(r, c):
        d = dest_ref[base + r]
        _row_copy(x_ref.at[pl.ds(r, 1)], o_hbm.at[pl.ds(d, 1)], sem).start()
        return c

    lax.fori_loop(0, ts, issue, 0, unroll=8)
    _row_copy(x_ref, o_hbm.at[pl.ds(0, ts)], sem).wait()


def _moe_scatter(dest, rows, mp, ts):
    m = rows.shape[0]
    init = jnp.zeros((mp, ROW_W), F32)
    grid_spec = pltpu.PrefetchScalarGridSpec(
        num_scalar_prefetch=1,
        grid=(m // ts,),
        in_specs=[
            pl.BlockSpec((ts, ROW_W), lambda i, d: (i, 0)),
            pl.BlockSpec(memory_space=pl.ANY),
        ],
        out_specs=pl.BlockSpec(memory_space=pl.ANY),
        scratch_shapes=[pltpu.SemaphoreType.DMA(())],
    )
    return pl.pallas_call(
        functools.partial(_scatter_body, ts=ts),
        grid_spec=grid_spec,
        out_shape=jax.ShapeDtypeStruct((mp, ROW_W), F32),
        input_output_aliases={2: 0},
        compiler_params=_cparams(("arbitrary",)),
        name="moe_scatter",
    )(dest, rows, init)


def _moe_body(blk_ref, val_ref, e1_ref, e2_ref, x_ref, nw_ref, wg1_ref, wu1_ref, wd1_ref, wg2_ref, wu2_ref, wd2_ref, o_ref):
    del blk_ref, e1_ref, e2_ref
    t = pl.program_id(0)

    @pl.when(val_ref[t] > 0)
    def _():
        x = x_ref[:, 0:D_MODEL]
        w_lo = x_ref[:, D_MODEL:D_MODEL + 1]
        w_hi = x_ref[:, D_MODEL + 1:D_MODEL + 2]
        tb = _rms(x, nw_ref[...]).astype(BF16)

        def expert(wg, wu, wd):
            hid = jax.nn.silu(_dot(tb, wg[0])) * _dot(tb, wu[0])
            return _dot(hid.astype(BF16), wd[0])

        moe = w_lo * expert(wg1_ref, wu1_ref, wd1_ref) + w_hi * expert(wg2_ref, wu2_ref, wd2_ref)
        o_ref[...] = x + moe

    @pl.when(val_ref[t] == 0)
    def _():
        o_ref[...] = jnp.zeros(o_ref.shape, F32)


def _moe_experts(tile_blk, tile_valid, tile_e1, tile_e2, xs, nw, wg, wu, wd, tmo):
    mp = xs.shape[0]
    nt = mp // tmo
    up = lambda sel: pl.BlockSpec((1, D_MODEL, EXPERT_HIDDEN), (lambda t, tb, tv, e1, e2: ((e1, e2)[sel][t], 0, 0)))
    down = lambda sel: pl.BlockSpec((1, EXPERT_HIDDEN, D_MODEL), (lambda t, tb, tv, e1, e2: ((e1, e2)[sel][t], 0, 0)))
    grid_spec = pltpu.PrefetchScalarGridSpec(
        num_scalar_prefetch=4,
        grid=(nt,),
        in_specs=[
            pl.BlockSpec((tmo, ROW_W), lambda t, tb, tv, e1, e2: (tb[t], 0)),
            pl.BlockSpec((1, D_MODEL), lambda t, tb, tv, e1, e2: (0, 0)),
            up(0), up(0), down(0), up(1), up(1), down(1),
        ],
        out_specs=pl.BlockSpec((tmo, D_MODEL), lambda t, tb, tv, e1, e2: (t, 0)),
    )
    return pl.pallas_call(
        _moe_body,
        grid_spec=grid_spec,
        out_shape=jax.ShapeDtypeStruct((mp, D_MODEL), F32),
        compiler_params=_cparams(("arbitrary",)),
        name="moe_experts",
    )(tile_blk, tile_valid, tile_e1, tile_e2, xs, nw, wg, wu, wd, wg, wu, wd)


def _gather_body(dest_ref, ys_hbm, fnw_ref, o_ref, sem, *, ts, final_norm):
    base = pl.program_id(0) * ts

    def issue(r, c):
        d = dest_ref[base + r]
        _row_copy(ys_hbm.at[pl.ds(d, 1)], o_ref.at[pl.ds(r, 1)], sem).start()
        return c

    lax.fori_loop(0, ts, issue, 0, unroll=8)
    _row_copy(ys_hbm.at[pl.ds(0, ts)], o_ref, sem).wait()
    if final_norm:
        o_ref[...] = _rms(o_ref[...], fnw_ref[...])


def _moe_gather(dest, ys, fnw, m, ts, final_norm):
    grid_spec = pltpu.PrefetchScalarGridSpec(
        num_scalar_prefetch=1,
        grid=(m // ts,),
        in_specs=[
            pl.BlockSpec(memory_space=pl.ANY),
            pl.BlockSpec((1, D_MODEL), lambda i, d: (0, 0)),
        ],
        out_specs=pl.BlockSpec((ts, D_MODEL), lambda i, d: (i, 0)),
        scratch_shapes=[pltpu.SemaphoreType.DMA(())],
    )
    return pl.pallas_call(
        functools.partial(_gather_body, ts=ts, final_norm=final_norm),
        grid_spec=grid_spec,
        out_shape=jax.ShapeDtypeStruct((m, D_MODEL), F32),
        compiler_params=_cparams(("arbitrary",)),
        name="moe_gather",
    )(dest, ys, fnw)


def _moe_plan(cls, tmo):
    m = cls.shape[0]
    nt = m // tmo + N_CLASSES
    onehot = (cls[:, None] == jnp.arange(N_CLASSES, dtype=jnp.int32)[None, :]).astype(jnp.int32)
    csum = jnp.cumsum(onehot, axis=0)
    rank = jnp.sum(csum * onehot, axis=1) - 1
    counts = csum[-1]
    ntiles = (counts + tmo - 1) // tmo
    tile_end = jnp.cumsum(ntiles)
    tile_start = tile_end - ntiles
    dest = jnp.sum(onehot * (tile_start * tmo)[None, :], axis=1) + rank
    total = tile_end[-1]
    tid = jnp.arange(nt, dtype=jnp.int32)
    tile_valid = (tid < total).astype(jnp.int32)
    tile_blk = jnp.minimum(tid, total - 1)
    tile_cls = jnp.minimum(jnp.sum((tile_blk[:, None] >= tile_end[None, :]).astype(jnp.int32), axis=1), N_CLASSES - 1)
    grp = tile_cls // 6
    pair = tile_cls % 6
    lo = jnp.asarray(_PAIR_LO, jnp.int32)[pair]
    hi = jnp.asarray(_PAIR_HI, jnp.int32)[pair]
    return dest.astype(jnp.int32), tile_blk, tile_valid, grp * EXPERTS_PER_GROUP + lo, grp * EXPERTS_PER_GROUP + hi


def _moe(rows, nw2, wg, wu, wd, fnw, ts, tmo, final_norm):
    m = rows.shape[0]
    cls = rows[:, D_MODEL + 2].astype(jnp.int32)
    dest, tile_blk, tile_valid, e1, e2 = _moe_plan(cls, tmo)
    mp = m + N_CLASSES * tmo
    xs = _moe_scatter(dest, rows, mp, ts)
    ys = _moe_experts(tile_blk, tile_valid, e1, e2, xs, nw2, wg, wu, wd, tmo)
    return _moe_gather(dest, ys, fnw, m, ts, final_norm)


def _pad_lanes(v, width):
    return jnp.pad(v, ((0, 0), (0, width - v.shape[1])))


def _layer_params(l, p):
    w_in = p["w_in"][l]
    wp = jnp.concatenate([w_in[:, :OFF_DT], _pad_lanes(w_in[:, OFF_DT:OFF_U], LANES), w_in[:, OFF_U:]], axis=1)
    lp = {"norm1": p["norm1_w"][l][None, :]}
    lp["wp"], lp["wp_lo"] = _split2_param(wp)
    lp["cw8"] = jnp.pad(p["conv_w"][l], ((0, 4), (0, 0)))
    lp["cb"] = p["conv_b"][l][None, :]
    lp["dtb"] = _pad_lanes(p["dt_bias"][l][None, :], LANES)
    lp["alog"] = _pad_lanes(p["a_log"][l][None, :], LANES)
    lp["dvec"] = jnp.repeat(p["ssd_d"][l], SSD_HEAD_DIM)[None, :]
    lp["ssd_nw"] = p["ssd_norm_w"][l][None, :]
    lam = lax.complex(p["s5_lambda_re"][l], p["s5_lambda_im"][l])
    step = jnp.exp(p["s5_log_step"][l])[:, None]
    lam_bar = jnp.exp(lam * step)
    b = lax.complex(p["s5_b_re"][l], p["s5_b_im"][l])
    b_bar = ((lam_bar - 1.0) / lam)[..., None] * b
    eye = jnp.eye(S5_GROUPS, dtype=F32)

    def bdiag_in(m_gnc):
        return jnp.einsum("gnc,gh->gchn", m_gnc, eye).reshape(S5_WIDTH, S5_N)

    def bdiag_out(m_gcn):
        return jnp.einsum("gcn,gh->gnhc", m_gcn, eye).reshape(S5_N, S5_WIDTH)

    bmat = jnp.concatenate([bdiag_in(b_bar.real), bdiag_in(b_bar.imag)], axis=1)
    cmat = jnp.concatenate([bdiag_out(p["s5_c_re"][l]), -bdiag_out(p["s5_c_im"][l])], axis=0)
    lp["bhi"], lp["blo"] = _split2_param(bmat)
    lp["chi"], lp["clo"] = _split2_param(cmat)
    a_re = lam_bar.real.reshape(1, S5_N)
    a_im = lam_bar.imag.reshape(1, S5_N)
    lp["are"] = jnp.concatenate([a_re, a_re], axis=1)
    lp["aim"] = jnp.concatenate([-a_im, a_im], axis=1)
    lp["s5_d"] = p["s5_d"][l][None, :]
    lp["wglu"], lp["wglu_lo"] = _split2_param(p["s5_w_glu"][l])
    lp["bglu"] = p["s5_b_glu"][l][None, :]
    lam_init = 0.8 - 0.6 * math.exp(-0.3 * l)
    lam_s = (jnp.exp(jnp.sum(p["lam_q1"][l] * p["lam_k1"][l])) - jnp.exp(jnp.sum(p["lam_q2"][l] * p["lam_k2"][l])) + lam_init)
    lp["lam"] = jnp.full((1, LANES), lam_s, F32)
    lp["lam_init"] = lam_init
    lp["dnw_col"] = p["da_norm_w"][l][:, None]
    lp["wo"], lp["wo_lo"] = _split2_param(p["w_out"][l])
    lp["norm2"] = p["norm2_w"][l][None, :]
    wr = jnp.zeros((D_MODEL, LANES), F32)
    wr = wr.at[:, 0:MOE_GROUPS].set(p["w_router_group"][l]).at[:, 16:16 + N_EXPERTS].set(p["w_router_expert"][l])
    lp["rhi"], lp["rlo"] = _split2_param(wr)
    rb = jnp.zeros((1, LANES), F32)
    lp["rb"] = rb.at[0, 0:MOE_GROUPS].set(p["b_router_group"][l]).at[0, 16:16 + N_EXPERTS].set(p["b_router_expert"][l])
    lp["wg"] = p["w_gate"][l].astype(BF16)
    lp["wu"] = p["w_up"][l].astype(BF16)
    lp["wd"] = p["w_down"][l].astype(BF16)
    return lp


def _mixers(x2d, nb, t_len, lp, conv_buf, ssd_h0, s5_s0, attend, tm_proj, hp, qscale):
    z, xbc, dt, u, k, v, *qkv = _inproj(x2d, lp["norm1"], lp["wp"], lp["wp_lo"], tm_proj, hp, qscale)
    L = SSD_CHUNK
    tp = -(-t_len // L) * L

    def seq(a):
        a = a.reshape(nb, t_len, a.shape[-1])
        return a if tp == t_len else jnp.pad(a, ((0, 0), (0, tp - t_len), (0, 0)))

    cbuf0 = jnp.pad(conv_buf, ((0, 0), (5, 0), (0, 0)))
    tri = jnp.tril(jnp.ones((L, L), F32)).astype(BF16)
    y_ssd, cnew, ssd_new = _ssd(seq(xbc), seq(z), seq(dt), cbuf0, ssd_h0, lp["cw8"], lp["cb"], lp["dtb"], lp["alog"],
                             lp["dvec"], lp["ssd_nw"], tri, t_len, hp)
    y_ssd = y_ssd[:, :t_len].reshape(nb * t_len, SSD_INNER)
    conv_new = cnew[:, 0:3]

    u_tb = jnp.swapaxes(u.reshape(nb, t_len, S5_WIDTH), 0, 1).reshape(t_len * nb, S5_WIDTH)
    s0 = jnp.concatenate([s5_s0[..., 0].reshape(nb, S5_N), s5_s0[..., 1].reshape(nb, S5_N)], axis=1)
    tc = min(t_len, 64)
    y5_tb, sfin = _s5(u_tb, s0, lp["are"], lp["aim"], lp["bhi"], lp["blo"], lp["chi"], lp["clo"], lp["s5_d"],
                      lp["wglu"], lp["wglu_lo"], lp["bglu"], nb, tc, hp)
    y_s5 = jnp.swapaxes(y5_tb.reshape(t_len, nb, S5_WIDTH), 0, 1).reshape(nb * t_len, S5_WIDTH)
    s5_new = jnp.stack([sfin[:, :S5_N].reshape(nb, S5_GROUPS, S5_STATE), sfin[:, S5_N:].reshape(nb, S5_GROUPS, S5_STATE)],
                       axis=-1)

    y_da = attend(k, v, *qkv)
    k_rows = k.reshape(nb, t_len, DA_HEADS, 2 * DA_DQK)
    v_rows = v.reshape(nb, t_len, DA_HEADS, DA_DV)
    return y_ssd, y_s5, y_da, (k_rows, v_rows, conv_new, ssd_new, s5_new)


def _trunk_layer(x2d, nb, t_len, lp, conv_buf, ssd_h0, s5_s0, attend, fnw, final_norm, tm_proj, ts, tmo, hp, qscale):
    y_ssd, y_s5, y_da, states = _mixers(x2d, nb, t_len, lp, conv_buf, ssd_h0, s5_s0, attend, tm_proj, hp, qscale)
    rows = _post(y_ssd, y_s5, y_da, x2d, lp["wo"], lp["wo_lo"], lp["norm2"], lp["rhi"], lp["rlo"], lp["rb"], tm_proj, hp)
    x_new = _moe(rows, lp["norm2"], lp["wg"], lp["wu"], lp["wd"], fnw, ts, tmo, final_norm)
    return x_new, states


def kernel(x_prompt, x_sample, cache_k, cache_v, page_table, state_conv, state_ssd, state_s5, norm1_w, w_in, conv_w, conv_b, dt_bias, a_log, ssd_d, ssd_norm_w, s5_lambda_re, s5_lambda_im, s5_log_step, s5_b_re, s5_b_im, s5_c_re, s5_c_im, s5_d, s5_w_glu, s5_b_glu, lam_q1, lam_k1, lam_q2, lam_k2, da_norm_w, w_out, norm2_w, w_router_group, b_router_group, w_router_expert, b_router_expert, w_gate, w_up, w_down, final_norm_w):
    params = dict(
        norm1_w=norm1_w, w_in=w_in, conv_w=conv_w, conv_b=conv_b, dt_bias=dt_bias, a_log=a_log, ssd_d=ssd_d,
        ssd_norm_w=ssd_norm_w, s5_lambda_re=s5_lambda_re, s5_lambda_im=s5_lambda_im, s5_log_step=s5_log_step,
        s5_b_re=s5_b_re, s5_b_im=s5_b_im, s5_c_re=s5_c_re, s5_c_im=s5_c_im, s5_d=s5_d, s5_w_glu=s5_w_glu,
        s5_b_glu=s5_b_glu, lam_q1=lam_q1, lam_k1=lam_k1, lam_q2=lam_q2, lam_k2=lam_k2, da_norm_w=da_norm_w,
        w_out=w_out, norm2_w=norm2_w, w_router_group=w_router_group, b_router_group=b_router_group,
        w_router_expert=w_router_expert, b_router_expert=b_router_expert, w_gate=w_gate, w_up=w_up, w_down=w_down)
    bp, t_p, _ = x_prompt.shape
    bs, t_s, _ = x_sample.shape
    slopes = jnp.exp2(-8.0 * jnp.arange(1, DA_HEADS + 1, dtype=F32) / DA_HEADS)
    slopes8 = jnp.broadcast_to(jnp.pad(slopes, (0, 2))[:, None], (8, LANES))
    cache_kt = jnp.transpose(cache_k, (0, 1, 3, 4, 2))
    cache_vt = jnp.transpose(cache_v, (0, 1, 3, 4, 2))
    fnw = final_norm_w[None, :]

    xp = x_prompt.reshape(bp * t_p, D_MODEL)
    xs = x_sample.reshape(bs * t_s, D_MODEL)
    p_states, s_states = [], []
    for l in range(DEPTH):
        lp = _layer_params(l, params)
        last = l == DEPTH - 1

        hp_prompt = l == 0

        def attend_p(k, v, *rest, lp=lp, hp=hp_prompt):
            r3 = lambda a: a.reshape(bp, t_p, DA_WIDTH)
            qf = r3(rest[0]) if hp else None
            qb, kb, vb = rest[-3:]
            return _attn_prompt(r3(qb), r3(kb), r3(vb), qf, r3(k), r3(v), slopes8, lp["lam"], lp["dnw_col"],
                                lp["lam_init"], 512, hp).reshape(bp * t_p, DA_WIDTH)

        def attend_s(k, v, q, *_, lp=lp, l=l):
            shp = (bs, DA_HEADS, DA_DV, 1)
            return _attn_decode(q.reshape(shp), k.reshape(shp), v.reshape(shp), cache_kt, cache_vt, page_table, slopes8,
                                lp["lam"], lp["dnw_col"], l, lp["lam_init"]).reshape(bs, DA_WIDTH)

        xp, st_p = _trunk_layer(
            xp, bp, t_p, lp,
            jnp.zeros((bp, SSD_CONV - 1, SSD_XBC), F32),
            jnp.zeros((bp, SSD_HEADS, SSD_HEAD_DIM, SSD_STATE), F32),
            jnp.zeros((bp, S5_GROUPS, S5_STATE, 2), F32),
            attend_p, fnw, last, 512, 256, 256, hp_prompt, C_QK)
        p_states.append(st_p)
        xs, st_s = _trunk_layer(xs, bs, t_s, lp, state_conv[l], state_ssd[l], state_s5[l], attend_s, fnw, last, 32, 32, 16, True, 1.0)
        s_states.append(st_s)
    k_p, v_p, conv_p, ssd_p, s5_p = [jnp.stack([st[i] for st in p_states]) for i in range(5)]
    k_s, v_s, conv_s, ssd_s, s5_s = [jnp.stack([st[i] for st in s_states]) for i in range(5)]
    y_prompt = xp.reshape(bp, t_p, D_MODEL)
    y_sample = xs.reshape(bs, t_s, D_MODEL)
    return (y_prompt, y_sample, k_p, v_p, k_s, v_s, conv_p, conv_s, ssd_p, ssd_s, s5_p, s5_s)
```
